```python
import jax, jax.numpy as jnp
from jax import lax
import numpy as np

D_MODEL = 4096
BATCH = 1
SEQ = 8192
DEPTH = 1

D_MIX = D_MODEL
D_POOL = D_MIX // 2
POOL_WINDOWS = (2, 4, 8, 16)
N_POOL_GROUPS = len(POOL_WINDOWS)
POOL_GROUP = D_POOL // N_POOL_GROUPS
D_GLA = D_MIX - D_POOL
GLA_HEADS = 4
GLA_DV = D_GLA // GLA_HEADS
GLA_DK = GLA_DV // 2
GLA_KEY = GLA_HEADS * GLA_DK
GLA_RANK = 16
GLA_TAU = 16.0
GLA_CHUNK = 64
EPS = 1e-6

IN_SIZES = (D_POOL, D_POOL, GLA_KEY, GLA_KEY, D_GLA, D_GLA, GLA_RANK)
D_IN = sum(IN_SIZES)

kernel_name = "hybrid_pool_gla_adaln_layer"


def _rmsnorm(x, w):
    xf = x.astype(jnp.float32)
    return xf * lax.rsqrt(jnp.mean(xf * xf, axis=-1, keepdims=True) + EPS) * w.astype(jnp.float32)


def _pool_mixer(u, w_pool, pool_scale):
    B, T, _ = u.shape
    ug = u.reshape(B, T, N_POOL_GROUPS, POOL_GROUP)
    cs = jnp.cumsum(ug, axis=1)
    t = jnp.arange(T)
    outs = []
    for g, w in enumerate(POOL_WINDOWS):
        c_g = cs[:, :, g]
        lag = jnp.pad(c_g[:, :T - w], ((0, 0), (w, 0), (0, 0)))
        cnt = jnp.minimum(t + 1, w).astype(jnp.float32)[None, :, None]
        outs.append((c_g - lag) / cnt - ug[:, :, g])
    pooled = jnp.stack(outs, axis=2)
    mixed = jnp.einsum('btgc,gcd->btgd', pooled, w_pool.astype(jnp.float32))
    return mixed.reshape(B, T, D_POOL) * pool_scale.astype(jnp.float32)


def _gla_chunked(q, k, v, log_a):
    B, T, H, dk = q.shape
    dv = v.shape[-1]
    C = GLA_CHUNK
    N = T // C

    def to_chunks(a):
        return a.reshape(B, N, C, H, a.shape[-1]).transpose(1, 0, 3, 2, 4)

    causal = jnp.tril(jnp.ones((C, C), dtype=bool))[:, :, None]

    def step(S, inp):
        qc, kc, vc, gc = inp
        b = jnp.cumsum(gc, axis=2)
        o_inter = jnp.einsum('bhcd,bhde->bhce', qc * jnp.exp(b), S)
        diff = b[:, :, :, None, :] - b[:, :, None, :, :]
        decay = jnp.exp(jnp.where(causal, diff, -jnp.inf))
        A = jnp.einsum('bhid,bhjd,bhijd->bhij', qc, kc, decay)
        o_intra = jnp.einsum('bhij,bhje->bhie', A, vc)
        b_last = b[:, :, -1:, :]
        k_dec = kc * jnp.exp(b_last - b)
        S_new = jnp.exp(b_last[:, :, 0, :])[..., None] * S + jnp.einsum('bhjd,bhje->bhde', k_dec, vc)
        return S_new, o_inter + o_intra

    S0 = jnp.zeros((B, H, dk, dv), jnp.float32)
    _, o = lax.scan(step, S0, (to_chunks(q), to_chunks(k), to_chunks(v), to_chunks(log_a)))
    return o.transpose(1, 0, 3, 2, 4).reshape(B, T, H, dv)


def setup_inputs(seed: int = 0) -> dict:
    key = jax.random.key(seed)
    ks = jax.random.split(key, 14)
    f32 = jnp.float32
    D = D_MODEL
    x = jax.random.normal(ks[0], (BATCH, SEQ, D), f32)
    c = jax.random.normal(ks[1], (BATCH, D), f32)
    w_ada = jax.random.normal(ks[2], (DEPTH, D, 3 * D), f32) * (0.5 * D ** -0.5)
    b_ada = jax.random.normal(ks[3], (DEPTH, 3 * D), f32) * 0.02
    norm_w = 1.0 + 0.02 * jax.random.normal(ks[4], (DEPTH, D), f32)
    w_in = jax.random.normal(ks[5], (DEPTH, D, D_IN), f32) * D ** -0.5
    w_pool = jax.random.normal(ks[6], (DEPTH, N_POOL_GROUPS, POOL_GROUP, POOL_GROUP), f32) * POOL_GROUP ** -0.5
    pool_scale = 1.0 + 0.02 * jax.random.normal(ks[7], (DEPTH, D_POOL), f32)
    w_alpha = jax.random.normal(ks[8], (DEPTH, GLA_RANK, GLA_KEY), f32) * GLA_RANK ** -0.5
    b_alpha = jax.random.normal(ks[9], (DEPTH, GLA_KEY), f32) * 0.02
    gla_norm_w = 1.0 + 0.02 * jax.random.normal(ks[10], (DEPTH, GLA_DV), f32)
    w_out = jax.random.normal(ks[11], (DEPTH, D_MIX, D), f32) * D_MIX ** -0.5
    final_norm_w = 1.0 + 0.02 * jax.random.normal(ks[12], (D,), f32)
    return {"x": x, "c": c, "w_ada": w_ada, "b_ada": b_ada, "norm_w": norm_w,
            "w_in": w_in, "w_pool": w_pool, "pool_scale": pool_scale,
            "w_alpha": w_alpha, "b_alpha": b_alpha, "gla_norm_w": gla_norm_w,
            "w_out": w_out, "final_norm_w": final_norm_w}


def reference(x, c, w_ada, b_ada, norm_w, w_in, w_pool, pool_scale, w_alpha, b_alpha,
              gla_norm_w, w_out, final_norm_w):
    B, T, D = x.shape
    in_dtype = x.dtype
    split_idx = [int(v) for v in np.cumsum(IN_SIZES)[:-1]]
    c_act = jax.nn.silu(c.astype(jnp.float32))
    for l in range(DEPTH):
        mod = c_act @ w_ada[l].astype(jnp.float32) + b_ada[l].astype(jnp.float32)
        shift, scale, gate = jnp.split(mod, 3, axis=-1)
        h = _rmsnorm(x, norm_w[l]) * (1.0 + scale[:, None, :]) + shift[:, None, :]
        z = jnp.einsum('btd,de->bte', h, w_in[l].astype(jnp.float32))
        u, g_pool, q, k, v, g_gla, a_lr = jnp.split(z, split_idx, axis=-1)

        y_pool = _pool_mixer(u, w_pool[l], pool_scale[l]) * jax.nn.silu(g_pool)

        log_a = jax.nn.log_sigmoid(a_lr @ w_alpha[l].astype(jnp.float32)
                                   + b_alpha[l].astype(jnp.float32)) / GLA_TAU
        qh = q.reshape(B, T, GLA_HEADS, GLA_DK) * (GLA_DK ** -0.5)
        kh = k.reshape(B, T, GLA_HEADS, GLA_DK)
        vh = v.reshape(B, T, GLA_HEADS, GLA_DV)
        ah = log_a.reshape(B, T, GLA_HEADS, GLA_DK)
        o = _gla_chunked(qh, kh, vh, ah)
        o = _rmsnorm(o, gla_norm_w[l]).reshape(B, T, D_GLA)
        y_gla = o * jax.nn.silu(g_gla)

        y = jnp.einsum('btm,md->btd', jnp.concatenate([y_pool, y_gla], axis=-1),
                       w_out[l].astype(jnp.float32))
        x = (x.astype(jnp.float32) + gate[:, None, :] * y).astype(in_dtype)
    return _rmsnorm(x, final_norm_w).astype(in_dtype)
```

```python
import functools

import jax
import jax.numpy as jnp
from jax import lax
from jax.experimental import pallas as pl
from jax.experimental.pallas import tpu as pltpu

F32 = jnp.float32
BF16 = jnp.bfloat16

D_MODEL = 4096
D_POOL = 2048
POOL_WINDOWS = (2, 4, 8, 16)
POOL_GROUP = 512
POOL_HALO = 16
D_GLA = 2048
GLA_HEADS = 4
GLA_DV = 512
GLA_DK = 256
GLA_KEY = 1024
GLA_RANK = 16
GLA_TAU = 16.0
GLA_CHUNK = 64
GLA_SUB = 16
EPS = 1e-6
D_MAIN = 2 * D_POOL + 2 * GLA_KEY + 2 * D_GLA
RANK_PAD = 128

OFF_U, OFF_GP, OFF_Q, OFF_K, OFF_V, OFF_GG = 0, 2048, 4096, 5120, 6144, 8192

VMEM_LIMIT = 56 * 1024 * 1024


def _dot(a, b):
    return jnp.dot(a, b, preferred_element_type=F32)


def _dot_nt(a, b):
    return lax.dot_general(a, b, (((1,), (1,)), ((), ())), preferred_element_type=F32)


def _dot_tn(a, b):
    return lax.dot_general(a, b, (((0,), (0,)), ((), ())), preferred_element_type=F32)


def _split_bf16(a):
    hi = a.astype(BF16)
    lo = (a - hi.astype(F32)).astype(BF16)
    return hi, lo


def _silu(v):
    return v * jax.nn.sigmoid(v)


ADA_TN = 512
ADA_KC = 256


def _adaln_kernel(c_ref, w_ref, b_ref, o_ref):
    tn = o_ref.shape[1]

    def body(kk, acc):
        r = pl.multiple_of(kk * ADA_KC, ADA_KC)
        cc = c_ref[pl.ds(r, ADA_KC), :]
        p = _silu(cc) * w_ref[pl.ds(r, ADA_KC), :]
        return acc + p.reshape(ADA_KC // 8, 8, tn).sum(axis=0)

    acc = lax.fori_loop(0, D_MODEL // ADA_KC, body, jnp.zeros((8, tn), F32))
    o_ref[...] = acc.sum(axis=0, keepdims=True) + b_ref[...]


def _adaln(c_col, w_ada, b_ada):
    n = w_ada.shape[1]
    return pl.pallas_call(
        _adaln_kernel,
        grid=(n // ADA_TN,),
        in_specs=[
            pl.BlockSpec((D_MODEL, 1), lambda j: (0, 0)),
            pl.BlockSpec((D_MODEL, ADA_TN), lambda j: (0, j)),
            pl.BlockSpec((1, ADA_TN), lambda j: (0, j)),
        ],
        out_specs=pl.BlockSpec((1, ADA_TN), lambda j: (0, j)),
        out_shape=jax.ShapeDtypeStruct((1, n), F32),
        compiler_params=pltpu.CompilerParams(
            dimension_semantics=("arbitrary",), vmem_limit_bytes=VMEM_LIMIT),
        name="adaln",
    )(c_col, w_ada, b_ada)


IN_TM = 512
IN_TN = 1024
IN_RC = 64


def _inproj_kernel(x_ref, nw_ref, shift_ref, scale_ref, w_ref, wa_ref, z_ref, a_ref, h_scr):
    j = pl.program_id(1)

    @pl.when(j == 0)
    def _():
        mul = nw_ref[...] * (1.0 + scale_ref[...])
        sh = shift_ref[...]

        def body(r, carry):
            r0 = pl.multiple_of(r * IN_RC, IN_RC)
            xs = x_ref[pl.ds(r0, IN_RC), :]
            ms = jnp.mean(xs * xs, axis=-1, keepdims=True)
            h = xs * lax.rsqrt(ms + EPS) * mul + sh
            h_scr[pl.ds(r0, IN_RC), :] = h.astype(BF16)
            return carry

        lax.fori_loop(0, IN_TM // IN_RC, body, 0)
        a_ref[...] = _dot(h_scr[...], wa_ref[...])

    z_ref[...] = _dot(h_scr[...], w_ref[...]).astype(z_ref.dtype)


def _inproj(x2, norm_w, mod, w_main, w_a):
    t = x2.shape[0]
    return pl.pallas_call(
        _inproj_kernel,
        grid=(t // IN_TM, D_MAIN // IN_TN),
        in_specs=[
            pl.BlockSpec((IN_TM, D_MODEL), lambda i, j: (i, 0)),
            pl.BlockSpec((1, D_MODEL), lambda i, j: (0, 0)),
            pl.BlockSpec((1, D_MODEL), lambda i, j: (0, 0)),
            pl.BlockSpec((1, D_MODEL), lambda i, j: (0, 1)),
            pl.BlockSpec((D_MODEL, IN_TN), lambda i, j: (0, j)),
            pl.BlockSpec((D_MODEL, RANK_PAD), lambda i, j: (0, 0)),
        ],
        out_specs=[
            pl.BlockSpec((IN_TM, IN_TN), lambda i, j: (i, j)),
            pl.BlockSpec((IN_TM, RANK_PAD), lambda i, j: (i, 0)),
        ],
        out_shape=[
            jax.ShapeDtypeStruct((t, D_MAIN), BF16),
            jax.ShapeDtypeStruct((t, RANK_PAD), F32),
        ],
        scratch_shapes=[pltpu.VMEM((IN_TM, D_MODEL), BF16)],
        compiler_params=pltpu.CompilerParams(
            dimension_semantics=("arbitrary", "arbitrary"), vmem_limit_bytes=VMEM_LIMIT),
        name="inproj",
    )(x2, norm_w, mod, mod, w_main, w_a)


POOL_TM = 256


def _pool_kernel(u_ref, up_ref, g_ref, w_ref, ps_ref, o_ref):
    i = pl.program_id(0)
    grp = pl.program_id(1)
    tm = POOL_TM
    win = jnp.left_shift(2, grp)

    row = lax.broadcasted_iota(jnp.int32, (tm, tm), 0)
    col = lax.broadcasted_iota(jnp.int32, (tm, tm), 1)
    band = jnp.logical_and(col <= row, col > row - win).astype(BF16)
    u = u_ref[...]
    wsum = _dot(band, u)

    hr = lax.broadcasted_iota(jnp.int32, (POOL_HALO, POOL_HALO), 0)
    hc = lax.broadcasted_iota(jnp.int32, (POOL_HALO, POOL_HALO), 1)
    hband = jnp.logical_and(hr - hc + POOL_HALO < win, i > 0).astype(BF16)
    hsum = _dot(hband, up_ref[...])
    pad = jnp.zeros((tm - POOL_HALO, POOL_GROUP), F32)
    wsum = wsum + jnp.concatenate([hsum, pad], axis=0)

    t_glob = i * tm + lax.broadcasted_iota(jnp.int32, (tm, 1), 0)
    cnt = jnp.minimum(t_glob + 1, win).astype(F32)
    pooled = wsum / cnt - u.astype(F32)
    mixed = _dot(pooled.astype(BF16), w_ref[0])
    y = mixed * ps_ref[...] * _silu(g_ref[...].astype(F32))
    o_ref[...] = y.astype(o_ref.dtype)


def _pool(z, w_pool_bf, pool_scale):
    t = z.shape[0]
    hb = POOL_TM // POOL_HALO
    gb = OFF_GP // POOL_GROUP
    return pl.pallas_call(
        _pool_kernel,
        grid=(t // POOL_TM, len(POOL_WINDOWS)),
        in_specs=[
            pl.BlockSpec((POOL_TM, POOL_GROUP), lambda i, g: (i, g)),
            pl.BlockSpec((POOL_HALO, POOL_GROUP), lambda i, g: (jnp.maximum(i * hb - 1, 0), g)),
            pl.BlockSpec((POOL_TM, POOL_GROUP), lambda i, g: (i, gb + g)),
            pl.BlockSpec((1, POOL_GROUP, POOL_GROUP), lambda i, g: (g, 0, 0)),
            pl.BlockSpec((1, POOL_GROUP), lambda i, g: (0, g)),
        ],
        out_specs=pl.BlockSpec((POOL_TM, POOL_GROUP), lambda i, g: (i, g)),
        out_shape=jax.ShapeDtypeStruct((t, D_POOL), BF16),
        compiler_params=pltpu.CompilerParams(
            dimension_semantics=("arbitrary", "arbitrary"), vmem_limit_bytes=VMEM_LIMIT),
        name="pool",
    )(z, z, z, w_pool_bf, pool_scale)


GLA_TR = 256


def _row_bcast(ref, row, n):
    return jnp.broadcast_to(ref[pl.ds(row, 1), :], (n, ref.shape[1]))


def _gla_kernel(q_ref, k_ref, v_ref, gg_ref, a_ref, wa_ref, ba_ref, nw_ref, o_ref,
                st_ref, b_scr, k_scr):
    tr, ch, sb = GLA_TR, GLA_CHUNK, GLA_SUB
    n_ch = tr // ch
    n_sb = tr // sb
    sb_per_ch = ch // sb

    @pl.when(pl.program_id(1) == 0)
    def _():
        st_ref[...] = jnp.zeros_like(st_ref)

    a_hi, a_lo = _split_bf16(a_ref[...])
    w_hi, w_lo = _split_bf16(wa_ref[...])
    xg = _dot(a_hi, w_hi) + _dot(a_lo, w_hi) + _dot(a_hi, w_lo) + ba_ref[...]
    g = (jnp.minimum(xg, 0.0) - jnp.log1p(jnp.exp(-jnp.abs(xg)))) / GLA_TAU

    row = lax.broadcasted_iota(jnp.int32, (tr, tr), 0)
    col = lax.broadcasted_iota(jnp.int32, (tr, tr), 1)
    ch_sh, sb_sh = ch.bit_length() - 1, sb.bit_length() - 1
    same_chunk = (row >> ch_sh) == (col >> ch_sh)
    tri = jnp.logical_and(same_chunk, col <= row).astype(BF16)
    g_hi, g_lo = _split_bf16(g)
    b = _dot(tri, g_hi) + _dot(tri, g_lo)
    b_scr[...] = b
    kf = k_ref[...].astype(F32)
    k_scr[...] = kf
    qf = q_ref[...].astype(F32) * (GLA_DK ** -0.5)
    v = v_ref[...]

    lane = lax.broadcasted_iota(jnp.int32, (tr, RANK_PAD), 1)
    rsub = lax.broadcasted_iota(jnp.int32, (tr, RANK_PAD), 0) & (sb - 1)
    a_diag = jnp.zeros((tr, RANK_PAD), F32)
    for j in range(sb):
        kj = jnp.concatenate([_row_bcast(k_scr, s * sb + j, sb) for s in range(n_sb)], axis=0)
        bj = jnp.concatenate([_row_bcast(b_scr, s * sb + j, sb) for s in range(n_sb)], axis=0)
        e = jnp.exp(jnp.minimum(b - bj, 0.0))
        colj = jnp.sum(qf * kj * e, axis=1, keepdims=True)
        a_diag = jnp.where(lane == j, colj, a_diag)
    a_diag = jnp.where(lane <= rsub, a_diag, 0.0)
    er = lax.broadcasted_iota(jnp.int32, (RANK_PAD, tr), 0)
    ec = lax.broadcasted_iota(jnp.int32, (RANK_PAD, tr), 1)
    expand = ((ec & (sb - 1)) == er).astype(BF16)
    a_full = _dot(a_diag.astype(BF16), expand)
    a_mat = jnp.where((row >> sb_sh) == (col >> sb_sh), a_full, 0.0)

    rsb = (row & (ch - 1)) >> sb_sh
    csb = (col & (ch - 1)) >> sb_sh
    for i in range(1, sb_per_ch):
        beta = jnp.concatenate(
            [_row_bcast(b_scr, c * ch + i * sb - 1, ch) for c in range(n_ch)], axis=0)
        qh = qf * jnp.exp(jnp.minimum(b - beta, 0.0))
        kh = kf * jnp.exp(jnp.minimum(beta - b, 0.0))
        p = _dot_nt(qh.astype(BF16), kh.astype(BF16))
        valid = jnp.logical_and(same_chunk, jnp.logical_and(rsb == i, csb < i))
        a_mat = jnp.where(valid, p, a_mat)

    o_intra = _dot(a_mat.astype(BF16), v)

    b_last = jnp.concatenate([_row_bcast(b_scr, c * ch + ch - 1, ch) for c in range(n_ch)], axis=0)
    q_in = (qf * jnp.exp(b)).astype(BF16)
    k_out = (kf * jnp.exp(b_last - b)).astype(BF16)
    st = st_ref[...]
    o_parts = []
    for c in range(n_ch):
        sl = slice(c * ch, (c + 1) * ch)
        o_parts.append(_dot_nt(q_in[sl], st.astype(BF16)))
        dec = jnp.exp(b_scr[pl.ds(c * ch + ch - 1, 1), :])
        st = dec * st + _dot_tn(v[sl], k_out[sl])
    st_ref[...] = st
    o = o_intra + jnp.concatenate(o_parts, axis=0)

    ms = jnp.mean(o * o, axis=-1, keepdims=True)
    on = o * lax.rsqrt(ms + EPS) * nw_ref[...]
    o_ref[...] = (on * _silu(gg_ref[...].astype(F32))).astype(o_ref.dtype)


def _gla(z, a_lr, w_alpha_pad, b_alpha, gla_norm_w):
    t = z.shape[0]
    qb, kb = OFF_Q // GLA_DK, OFF_K // GLA_DK
    vb, gb = OFF_V // GLA_DV, OFF_GG // GLA_DV
    return pl.pallas_call(
        _gla_kernel,
        grid=(GLA_HEADS, t // GLA_TR),
        in_specs=[
            pl.BlockSpec((GLA_TR, GLA_DK), lambda h, i: (i, qb + h)),
            pl.BlockSpec((GLA_TR, GLA_DK), lambda h, i: (i, kb + h)),
            pl.BlockSpec((GLA_TR, GLA_DV), lambda h, i: (i, vb + h)),
            pl.BlockSpec((GLA_TR, GLA_DV), lambda h, i: (i, gb + h)),
            pl.BlockSpec((GLA_TR, RANK_PAD), lambda h, i: (i, 0)),
            pl.BlockSpec((RANK_PAD, GLA_DK), lambda h, i: (0, h)),
            pl.BlockSpec((1, GLA_DK), lambda h, i: (0, h)),
            pl.BlockSpec((1, GLA_DV), lambda h, i: (0, 0)),
        ],
        out_specs=pl.BlockSpec((GLA_TR, GLA_DV), lambda h, i: (i, h)),
        out_shape=jax.ShapeDtypeStruct((t, D_GLA), BF16),
        scratch_shapes=[
            pltpu.VMEM((GLA_DV, GLA_DK), F32),
            pltpu.VMEM((GLA_TR, GLA_DK), F32),
            pltpu.VMEM((GLA_TR, GLA_DK), F32),
        ],
        compiler_params=pltpu.CompilerParams(
            dimension_semantics=("arbitrary", "arbitrary"), vmem_limit_bytes=VMEM_LIMIT),
        name="gla",
    )(z, z, z, z, a_lr, w_alpha_pad, b_alpha, gla_norm_w)


OUT_TM = 512
OUT_TN = 1024


def _outproj_kernel(yp_ref, yg_ref, wp_ref, wg_ref, x_ref, gate_ref, fw_ref, o_ref):
    j = pl.program_id(1)
    y = _dot(yp_ref[...], wp_ref[...]) + _dot(yg_ref[...], wg_ref[...])
    c0 = pl.multiple_of(j * OUT_TN, OUT_TN)
    o_ref[:, pl.ds(c0, OUT_TN)] = x_ref[...] + gate_ref[...] * y

    @pl.when(j == pl.num_programs(1) - 1)
    def _():
        def body(r, carry):
            r0 = pl.multiple_of(r * IN_RC, IN_RC)
            xs = o_ref[pl.ds(r0, IN_RC), :]
            ms = jnp.mean(xs * xs, axis=-1, keepdims=True)
            o_ref[pl.ds(r0, IN_RC), :] = xs * lax.rsqrt(ms + EPS) * fw_ref[...]
            return carry

        lax.fori_loop(0, OUT_TM // IN_RC, body, 0)


def _outproj(y_pool, y_gla, w_out_p, w_out_g, x2, mod, final_norm_w):
    t = x2.shape[0]
    gate_blk = 2 * D_MODEL // OUT_TN
    return pl.pallas_call(
        _outproj_kernel,
        grid=(t // OUT_TM, D_MODEL // OUT_TN),
        in_specs=[
            pl.BlockSpec((OUT_TM, D_POOL), lambda i, j: (i, 0)),
            pl.BlockSpec((OUT_TM, D_GLA), lambda i, j: (i, 0)),
            pl.BlockSpec((D_POOL, OUT_TN), lambda i, j: (0, j)),
            pl.BlockSpec((D_GLA, OUT_TN), lambda i, j: (0, j)),
            pl.BlockSpec((OUT_TM, OUT_TN), lambda i, j: (i, j)),
            pl.BlockSpec((1, OUT_TN), lambda i, j: (0, gate_blk + j)),
            pl.BlockSpec((1, D_MODEL), lambda i, j: (0, 0)),
        ],
        out_specs=pl.BlockSpec((OUT_TM, D_MODEL), lambda i, j: (i, 0)),
        out_shape=jax.ShapeDtypeStruct((t, D_MODEL), F32),
        compiler_params=pltpu.CompilerParams(
            dimension_semantics=("arbitrary", "arbitrary"), vmem_limit_bytes=VMEM_LIMIT),
        name="outproj",
    )(y_pool, y_gla, w_out_p, w_out_g, x2, mod, final_norm_w)


def kernel(x, c, w_ada, b_ada, norm_w, w_in, w_pool, pool_scale, w_alpha, b_alpha,
           gla_norm_w, w_out, final_norm_w):
    bsz, t, d = x.shape
    assert bsz == 1 and d == D_MODEL and w_ada.shape[0] == 1
    x2 = x.reshape(t, d)

    mod = _adaln(c.reshape(d, 1), w_ada[0], b_ada)

    w_main = w_in[0, :, :D_MAIN].astype(BF16)
    w_a = jnp.pad(w_in[0, :, D_MAIN:], ((0, 0), (0, RANK_PAD - GLA_RANK))).astype(BF16)
    z, a_lr = _inproj(x2, norm_w, mod, w_main, w_a)

    y_pool = _pool(z, w_pool[0].astype(BF16), pool_scale)

    w_alpha_pad = jnp.pad(w_alpha[0], ((0, RANK_PAD - GLA_RANK), (0, 0)))
    y_gla = _gla(z, a_lr, w_alpha_pad, b_alpha, gla_norm_w)

    w_out_bf = w_out[0].astype(BF16)
    out = _outproj(y_pool, y_gla, w_out_bf[:D_POOL], w_out_bf[D_POOL:], x2, mod,
                   final_norm_w.reshape(1, d))
    return out.reshape(bsz, t, d).astype(x.dtype)
```

```python
import functools

import jax
import jax.numpy as jnp
import numpy as np
from jax import lax
from jax.experimental import pallas as pl
from jax.experimental.pallas import tpu as pltpu

F32 = jnp.float32
BF16 = jnp.bfloat16

D_MODEL = 4096
D_POOL = 2048
POOL_WINDOWS = (2, 4, 8, 16)
POOL_GROUP = 512
POOL_HALO = 16
D_GLA = 2048
GLA_HEADS = 4
GLA_DV = 512
GLA_DK = 256
GLA_KEY = 1024
GLA_RANK = 16
GLA_TAU = 16.0
EPS = 1e-6
D_MAIN = 2 * D_POOL + 2 * GLA_KEY + 2 * D_GLA
RANK_PAD = 128

OFF_U, OFF_GP, OFF_Q, OFF_K, OFF_V, OFF_GG = 0, 2048, 4096, 5120, 6144, 8192

VMEM_LIMIT = 56 * 1024 * 1024


def _dot(a, b):
    return jnp.dot(a, b, preferred_element_type=F32)


def _dot_nt(a, b):
    return lax.dot_general(a, b, (((1,), (1,)), ((), ())), preferred_element_type=F32)


def _dot_tn(a, b):
    return lax.dot_general(a, b, (((0,), (0,)), ((), ())), preferred_element_type=F32)


def _split_bf16(a):
    hi = a.astype(BF16)
    lo = (a - hi.astype(F32)).astype(BF16)
    return hi, lo


def _silu(v):
    return v * jax.nn.sigmoid(v)


ADA_TN = 512
ADA_KC = 256


def _adaln_kernel(c_ref, w_ref, b_ref, o_ref):
    tn = o_ref.shape[1]

    def body(kk, acc):
        r = pl.multiple_of(kk * ADA_KC, ADA_KC)
        cc = c_ref[pl.ds(r, ADA_KC), :]
        p = _silu(cc) * w_ref[pl.ds(r, ADA_KC), :]
        return acc + p.reshape(ADA_KC // 8, 8, tn).sum(axis=0)

    acc = lax.fori_loop(0, D_MODEL // ADA_KC, body, jnp.zeros((8, tn), F32))
    o_ref[...] = acc.sum(axis=0, keepdims=True) + b_ref[...]


def _adaln(c_col, w_ada, b_ada):
    n = w_ada.shape[1]
    return pl.pallas_call(
        _adaln_kernel,
        grid=(n // ADA_TN,),
        in_specs=[
            pl.BlockSpec((D_MODEL, 1), lambda j: (0, 0)),
            pl.BlockSpec((D_MODEL, ADA_TN), lambda j: (0, j)),
            pl.BlockSpec((1, ADA_TN), lambda j: (0, j)),
        ],
        out_specs=pl.BlockSpec((1, ADA_TN), lambda j: (0, j)),
        out_shape=jax.ShapeDtypeStruct((1, n), F32),
        compiler_params=pltpu.CompilerParams(
            dimension_semantics=("arbitrary",), vmem_limit_bytes=VMEM_LIMIT),
        name="adaln",
    )(c_col, w_ada, b_ada)


IN_TM = 512
IN_TN = 1024
IN_RC = 64


def _inproj_kernel(x_ref, nw_ref, shift_ref, scale_ref, w_ref, wa_ref, z_ref, a_ref, h_scr):
    j = pl.program_id(1)

    @pl.when(j == 0)
    def _():
        mul = nw_ref[...] * (1.0 + scale_ref[...])
        sh = shift_ref[...]

        def body(r, carry):
            r0 = pl.multiple_of(r * IN_RC, IN_RC)
            xs = x_ref[pl.ds(r0, IN_RC), :]
            ms = jnp.mean(xs * xs, axis=-1, keepdims=True)
            h = xs * lax.rsqrt(ms + EPS) * mul + sh
            h_scr[pl.ds(r0, IN_RC), :] = h.astype(BF16)
            return carry

        lax.fori_loop(0, IN_TM // IN_RC, body, 0)
        a_ref[...] = _dot(h_scr[...], wa_ref[...])

    z_ref[...] = _dot(h_scr[...], w_ref[...]).astype(z_ref.dtype)


def _inproj(x2, norm_w, mod, w_main, w_a):
    t = x2.shape[0]
    return pl.pallas_call(
        _inproj_kernel,
        grid=(t // IN_TM, D_MAIN // IN_TN),
        in_specs=[
            pl.BlockSpec((IN_TM, D_MODEL), lambda i, j: (i, 0)),
            pl.BlockSpec((1, D_MODEL), lambda i, j: (0, 0)),
            pl.BlockSpec((1, D_MODEL), lambda i, j: (0, 0)),
            pl.BlockSpec((1, D_MODEL), lambda i, j: (0, 1)),
            pl.BlockSpec((D_MODEL, IN_TN), lambda i, j: (0, j)),
            pl.BlockSpec((D_MODEL, RANK_PAD), lambda i, j: (0, 0)),
        ],
        out_specs=[
            pl.BlockSpec((IN_TM, IN_TN), lambda i, j: (i, j)),
            pl.BlockSpec((IN_TM, RANK_PAD), lambda i, j: (i, 0)),
        ],
        out_shape=[
            jax.ShapeDtypeStruct((t, D_MAIN), BF16),
            jax.ShapeDtypeStruct((t, RANK_PAD), F32),
        ],
        scratch_shapes=[pltpu.VMEM((IN_TM, D_MODEL), BF16)],
        compiler_params=pltpu.CompilerParams(
            dimension_semantics=("arbitrary", "arbitrary"), vmem_limit_bytes=VMEM_LIMIT),
        name="inproj",
    )(x2, norm_w, mod, mod, w_main, w_a)


POOL_TM = 1024
POOL_SUB = 256


def _pool_kernel(u_ref, up_ref, g_ref, w_ref, ps_ref, o_ref):
    grp = pl.program_id(0)
    i = pl.program_id(1)
    ts = POOL_SUB
    win = jnp.left_shift(2, grp)

    row = lax.broadcasted_iota(jnp.int32, (ts, ts), 0)
    col = lax.broadcasted_iota(jnp.int32, (ts, ts), 1)
    band = jnp.logical_and(col <= row, col > row - win).astype(BF16)
    hr = lax.broadcasted_iota(jnp.int32, (POOL_HALO, POOL_HALO), 0)
    hc = lax.broadcasted_iota(jnp.int32, (POOL_HALO, POOL_HALO), 1)
    hband = hr - hc + POOL_HALO < win
    pad = jnp.zeros((ts - POOL_HALO, POOL_GROUP), F32)
    w = w_ref[0]
    ps = ps_ref[...]

    for s in range(POOL_TM // ts):
        r0 = s * ts
        u = u_ref[r0:r0 + ts, :]
        if s == 0:
            halo = up_ref[...]
            hb = jnp.logical_and(hband, i > 0).astype(BF16)
        else:
            halo = u_ref[r0 - POOL_HALO:r0, :]
            hb = hband.astype(BF16)
        hsum = _dot(hb, halo)
        wsum = _dot(band, u) + jnp.concatenate([hsum, pad], axis=0)
        t_glob = i * POOL_TM + r0 + lax.broadcasted_iota(jnp.int32, (ts, 1), 0)
        cnt = jnp.minimum(t_glob + 1, win).astype(F32)
        pooled = wsum * (1.0 / cnt) - u.astype(F32)
        mixed = _dot(pooled.astype(BF16), w)
        y = mixed * ps * _silu(g_ref[r0:r0 + ts, :].astype(F32))
        o_ref[r0:r0 + ts, :] = y.astype(o_ref.dtype)


def _pool(z, w_pool_bf, pool_scale):
    t = z.shape[0]
    hb = POOL_TM // POOL_HALO
    gb = OFF_GP // POOL_GROUP
    return pl.pallas_call(
        _pool_kernel,
        grid=(len(POOL_WINDOWS), t // POOL_TM),
        in_specs=[
            pl.BlockSpec((POOL_TM, POOL_GROUP), lambda g, i: (i, g)),
            pl.BlockSpec((POOL_HALO, POOL_GROUP), lambda g, i: (jnp.maximum(i * hb - 1, 0), g)),
            pl.BlockSpec((POOL_TM, POOL_GROUP), lambda g, i: (i, gb + g)),
            pl.BlockSpec((1, POOL_GROUP, POOL_GROUP), lambda g, i: (g, 0, 0)),
            pl.BlockSpec((1, POOL_GROUP), lambda g, i: (0, g)),
        ],
        out_specs=pl.BlockSpec((POOL_TM, POOL_GROUP), lambda g, i: (i, g)),
        out_shape=jax.ShapeDtypeStruct((t, D_POOL), BF16),
        compiler_params=pltpu.CompilerParams(
            dimension_semantics=("arbitrary", "arbitrary"), vmem_limit_bytes=VMEM_LIMIT),
        name="pool",
    )(z, z, z, w_pool_bf, pool_scale)


GLA_TR = 256
GLA_LEVELS = GLA_TR.bit_length() - 1


def _gla_constants():
    r = np.arange(GLA_TR)[:, None]
    t = np.arange(GLA_TR)[None, :]
    mats = []
    for k in range(GLA_LEVELS):
        half = ((r >> (k + 1)) << (k + 1)) + (1 << k)
        second = ((r >> k) & 1) == 1
        mats.append(np.where(second, (t >= half) & (t <= r), (t > r) & (t < half)))
    mats.append(t <= r)
    tcat = np.concatenate(mats, axis=0).astype(np.float32)
    diff = np.maximum(r ^ t, 1)
    level = np.where(t == r, GLA_LEVELS, np.where(t < r, np.floor(np.log2(diff)).astype(np.int64), 127))
    return jnp.asarray(tcat, BF16), jnp.asarray(level, jnp.int32)


def _gla_kernel(q_ref, k_ref, v_ref, gg_ref, a_ref, wa_ref, ba_ref, nw_ref, tc_ref, lv_ref,
                o_ref, st_ref):
    tr, nl = GLA_TR, GLA_LEVELS

    @pl.when(pl.program_id(1) == 0)
    def _():
        st_ref[...] = jnp.zeros_like(st_ref)

    a_hi, a_lo = _split_bf16(a_ref[...])
    w_hi, w_lo = _split_bf16(wa_ref[...])
    xg = _dot(a_hi, w_hi) + _dot(a_lo, w_hi) + _dot(a_hi, w_lo) + ba_ref[...]
    g = (jnp.minimum(xg, 0.0) - jnp.log1p(jnp.exp(-jnp.abs(xg)))) / GLA_TAU

    g_hi, g_lo = _split_bf16(g)
    tcat = tc_ref[...]
    dall = _dot(tcat, g_hi) + _dot(tcat, g_lo)

    kf = k_ref[...].astype(F32)
    qf = q_ref[...].astype(F32) * (GLA_DK ** -0.5)
    v = v_ref[...]
    level = lv_ref[...]
    rbit = lax.broadcasted_iota(jnp.int32, (tr, 1), 0)

    a_mat = jnp.where(level == nl, _dot_nt(qf.astype(BF16), k_ref[...]), 0.0)
    for k in range(nl):
        d = dall[k * tr:(k + 1) * tr]
        side = jnp.where(((rbit >> k) & 1) == 1, qf, kf)
        m = (side * jnp.exp(d)).astype(BF16)
        a_mat = jnp.where(level == k, _dot_nt(m, m), a_mat)

    b = dall[nl * tr:]
    b_end = b[tr - 1:tr]
    q_in = (qf * jnp.exp(b)).astype(BF16)
    k_out = (kf * jnp.exp(b_end - b)).astype(BF16)
    st = st_ref[...]
    o = _dot(a_mat.astype(BF16), v) + _dot_nt(q_in, st.astype(BF16))
    st_ref[...] = jnp.exp(b_end) * st + _dot_tn(v, k_out)

    ms = jnp.mean(o * o, axis=-1, keepdims=True)
    on = o * lax.rsqrt(ms + EPS) * nw_ref[...]
    o_ref[...] = (on * _silu(gg_ref[...].astype(F32))).astype(o_ref.dtype)


def _gla(z, a_lr, w_alpha_pad, b_alpha, gla_norm_w):
    t = z.shape[0]
    tcat, level = _gla_constants()
    qb, kb = OFF_Q // GLA_DK, OFF_K // GLA_DK
    vb, gb = OFF_V // GLA_DV, OFF_GG // GLA_DV
    return pl.pallas_call(
        _gla_kernel,
        grid=(GLA_HEADS, t // GLA_TR),
        in_specs=[
            pl.BlockSpec((GLA_TR, GLA_DK), lambda h, i: (i, qb + h)),
            pl.BlockSpec((GLA_TR, GLA_DK), lambda h, i: (i, kb + h)),
            pl.BlockSpec((GLA_TR, GLA_DV), lambda h, i: (i, vb + h)),
            pl.BlockSpec((GLA_TR, GLA_DV), lambda h, i: (i, gb + h)),
            pl.BlockSpec((GLA_TR, RANK_PAD), lambda h, i: (i, 0)),
            pl.BlockSpec((RANK_PAD, GLA_DK), lambda h, i: (0, h)),
            pl.BlockSpec((1, GLA_DK), lambda h, i: (0, h)),
            pl.BlockSpec((1, GLA_DV), lambda h, i: (0, 0)),
            pl.BlockSpec(tcat.shape, lambda h, i: (0, 0)),
            pl.BlockSpec(level.shape, lambda h, i: (0, 0)),
        ],
        out_specs=pl.BlockSpec((GLA_TR, GLA_DV), lambda h, i: (i, h)),
        out_shape=jax.ShapeDtypeStruct((t, D_GLA), BF16),
        scratch_shapes=[pltpu.VMEM((GLA_DV, GLA_DK), F32)],
        compiler_params=pltpu.CompilerParams(
            dimension_semantics=("arbitrary", "arbitrary"), vmem_limit_bytes=VMEM_LIMIT),
        name="gla",
    )(z, z, z, z, a_lr, w_alpha_pad, b_alpha, gla_norm_w, tcat, level)


OUT_TM = 512
OUT_TN = 1024


def _outproj_kernel(yp_ref, yg_ref, wp_ref, wg_ref, x_ref, gate_ref, fw_ref, o_ref):
    j = pl.program_id(1)
    y = _dot(yp_ref[...], wp_ref[...]) + _dot(yg_ref[...], wg_ref[...])
    c0 = pl.multiple_of(j * OUT_TN, OUT_TN)
    o_ref[:, pl.ds(c0, OUT_TN)] = x_ref[...] + gate_ref[...] * y

    @pl.when(j == pl.num_programs(1) - 1)
    def _():
        def body(r, carry):
            r0 = pl.multiple_of(r * IN_RC, IN_RC)
            xs = o_ref[pl.ds(r0, IN_RC), :]
            ms = jnp.mean(xs * xs, axis=-1, keepdims=True)
            o_ref[pl.ds(r0, IN_RC), :] = xs * lax.rsqrt(ms + EPS) * fw_ref[...]
            return carry

        lax.fori_loop(0, OUT_TM // IN_RC, body, 0)


def _outproj(y_pool, y_gla, w_out_bf, x2, mod, final_norm_w):
    t = x2.shape[0]
    gate_blk = 2 * D_MODEL // OUT_TN
    return pl.pallas_call(
        _outproj_kernel,
        grid=(t // OUT_TM, D_MODEL // OUT_TN),
        in_specs=[
            pl.BlockSpec((OUT_TM, D_POOL), lambda i, j: (i, 0)),
            pl.BlockSpec((OUT_TM, D_GLA), lambda i, j: (i, 0)),
            pl.BlockSpec((D_POOL, OUT_TN), lambda i, j: (0, j)),
            pl.BlockSpec((D_GLA, OUT_TN), lambda i, j: (1, j)),
            pl.BlockSpec((OUT_TM, OUT_TN), lambda i, j: (i, j)),
            pl.BlockSpec((1, OUT_TN), lambda i, j: (0, gate_blk + j)),
            pl.BlockSpec((1, D_MODEL), lambda i, j: (0, 0)),
        ],
        out_specs=pl.BlockSpec((OUT_TM, D_MODEL), lambda i, j: (i, 0)),
        out_shape=jax.ShapeDtypeStruct((t, D_MODEL), F32),
        compiler_params=pltpu.CompilerParams(
            dimension_semantics=("arbitrary", "arbitrary"), vmem_limit_bytes=VMEM_LIMIT),
        name="outproj",
    )(y_pool, y_gla, w_out_bf, w_out_bf, x2, mod, final_norm_w)


def kernel(x, c, w_ada, b_ada, norm_w, w_in, w_pool, pool_scale, w_alpha, b_alpha,
           gla_norm_w, w_out, final_norm_w):
    bsz, t, d = x.shape
    assert bsz == 1 and d == D_MODEL and w_ada.shape[0] == 1
    x2 = x.reshape(t, d)

    mod = _adaln(c.reshape(d, 1), w_ada[0], b_ada)

    w_in_bf = w_in[0].astype(BF16)
    w_a = jnp.pad(w_in[0, :, D_MAIN:], ((0, 0), (0, RANK_PAD - GLA_RANK))).astype(BF16)
    z, a_lr = _inproj(x2, norm_w, mod, w_in_bf, w_a)

    y_pool = _pool(z, w_pool[0].astype(BF16), pool_scale)

    w_alpha_pad = jnp.pad(w_alpha[0], ((0, RANK_PAD - GLA_RANK), (0, 0)))
    y_gla = _gla(z, a_lr, w_alpha_pad, b_alpha, gla_norm_w)

    w_out_bf = w_out[0].astype(BF16)
    out = _outproj(y_pool, y_gla, w_out_bf, x2, mod,
                   final_norm_w.reshape(1, d))
    return out.reshape(bsz, t, d).astype(x.dtype)
```

```python
import math

import jax
import jax.numpy as jnp
import numpy as np
from jax import lax
from jax.experimental import pallas as pl
from jax.experimental.pallas import tpu as pltpu

F32 = jnp.float32
BF16 = jnp.bfloat16

D_MODEL = 4096
D_POOL = 2048
POOL_WINDOWS = (2, 4, 8, 16)
POOL_GROUP = 512
POOL_HALO = 16
D_GLA = 2048
GLA_HEADS = 4
GLA_DV = 512
GLA_DK = 256
GLA_KEY = 1024
GLA_RANK = 16
GLA_TAU = 16.0
EPS = 1e-6
D_MAIN = 2 * D_POOL + 2 * GLA_KEY + 2 * D_GLA
RANK_PAD = 128
LOG2E = math.log2(math.e)

OFF_U, OFF_GP, OFF_Q, OFF_K, OFF_V, OFF_GG = 0, 2048, 4096, 5120, 6144, 8192

VMEM_LIMIT = 56 * 1024 * 1024


def _dot(a, b):
    return jnp.dot(a, b, preferred_element_type=F32)


def _dot_nt(a, b):
    return lax.dot_general(a, b, (((1,), (1,)), ((), ())), preferred_element_type=F32)


def _dot_tn(a, b):
    return lax.dot_general(a, b, (((0,), (0,)), ((), ())), preferred_element_type=F32)


def _split_bf16(a):
    hi = a.astype(BF16)
    lo = (a - hi.astype(F32)).astype(BF16)
    return hi, lo


def _silu(v):
    return v * jax.nn.sigmoid(v)


def _params(n_axes, vmem_limit=VMEM_LIMIT):
    return pltpu.CompilerParams(
        dimension_semantics=("arbitrary",) * n_axes, vmem_limit_bytes=vmem_limit)


ADA_TN = 512
ADA_KC = 256


def _adaln_kernel(c_ref, w_ref, b_ref, o_ref):
    tn = o_ref.shape[1]

    def body(kk, acc):
        r = pl.multiple_of(kk * ADA_KC, ADA_KC)
        cc = c_ref[pl.ds(r, ADA_KC), :]
        p = _silu(cc) * w_ref[pl.ds(r, ADA_KC), :]
        return acc + p.reshape(ADA_KC // 8, 8, tn).sum(axis=0)

    acc = lax.fori_loop(0, D_MODEL // ADA_KC, body, jnp.zeros((8, tn), F32))
    o_ref[...] = acc.sum(axis=0, keepdims=True) + b_ref[...]


def _adaln(c_col, w_ada, b_ada):
    n = w_ada.shape[1]
    return pl.pallas_call(
        _adaln_kernel,
        grid=(n // ADA_TN,),
        in_specs=[
            pl.BlockSpec((D_MODEL, 1), lambda j: (0, 0)),
            pl.BlockSpec((D_MODEL, ADA_TN), lambda j: (0, j)),
            pl.BlockSpec((1, ADA_TN), lambda j: (0, j)),
        ],
        out_specs=pl.BlockSpec((1, ADA_TN), lambda j: (0, j)),
        out_shape=jax.ShapeDtypeStruct((1, n), F32),
        compiler_params=_params(1),
        name="adaln",
    )(c_col, w_ada, b_ada)


CAST_TN = 512


def _castw_kernel(w_ref, o_ref):
    o_ref[...] = w_ref[...].astype(o_ref.dtype)


def _castw(w):
    rows = w.shape[0]
    return pl.pallas_call(
        _castw_kernel,
        grid=(D_MAIN // CAST_TN,),
        in_specs=[pl.BlockSpec((rows, CAST_TN), lambda j: (0, j))],
        out_specs=pl.BlockSpec((rows, CAST_TN), lambda j: (0, j)),
        out_shape=jax.ShapeDtypeStruct((rows, D_MAIN), BF16),
        compiler_params=_params(1),
        name="castw",
    )(w)


NORM_TM = 512
NORM_RC = 64


def _norm_kernel(x_ref, nw_ref, shift_ref, scale_ref, wa_ref, h_ref, a_ref):
    mul = nw_ref[...] * (1.0 + scale_ref[...])
    sh = shift_ref[...]

    def body(r, carry):
        r0 = pl.multiple_of(r * NORM_RC, NORM_RC)
        xs = x_ref[pl.ds(r0, NORM_RC), :]
        ms = jnp.mean(xs * xs, axis=-1, keepdims=True)
        h_ref[pl.ds(r0, NORM_RC), :] = (xs * lax.rsqrt(ms + EPS) * mul + sh).astype(BF16)
        return carry

    lax.fori_loop(0, NORM_TM // NORM_RC, body, 0)
    a_ref[...] = _dot(h_ref[...], wa_ref[...])


def _norm(x2, norm_w, mod, w_a):
    t = x2.shape[0]
    return pl.pallas_call(
        _norm_kernel,
        grid=(t // NORM_TM,),
        in_specs=[
            pl.BlockSpec((NORM_TM, D_MODEL), lambda i: (i, 0)),
            pl.BlockSpec((1, D_MODEL), lambda i: (0, 0)),
            pl.BlockSpec((1, D_MODEL), lambda i: (0, 0)),
            pl.BlockSpec((1, D_MODEL), lambda i: (0, 1)),
            pl.BlockSpec((D_MODEL, RANK_PAD), lambda i: (0, 0)),
        ],
        out_specs=[
            pl.BlockSpec((NORM_TM, D_MODEL), lambda i: (i, 0)),
            pl.BlockSpec((NORM_TM, RANK_PAD), lambda i: (i, 0)),
        ],
        out_shape=[
            jax.ShapeDtypeStruct((t, D_MODEL), BF16),
            jax.ShapeDtypeStruct((t, RANK_PAD), F32),
        ],
        compiler_params=_params(1),
        name="norm",
    )(x2, norm_w, mod, mod, w_a)


IN_TM = 1024
IN_TN = 1024


def _inproj_kernel(h_ref, w_ref, z_ref):
    z_ref[...] = _dot(h_ref[...], w_ref[...]).astype(z_ref.dtype)


def _inproj(h, w_main):
    t = h.shape[0]
    return pl.pallas_call(
        _inproj_kernel,
        grid=(t // IN_TM, D_MAIN // IN_TN),
        in_specs=[
            pl.BlockSpec((IN_TM, D_MODEL), lambda i, j: (i, 0)),
            pl.BlockSpec((D_MODEL, IN_TN), lambda i, j: (0, j)),
        ],
        out_specs=pl.BlockSpec((IN_TM, IN_TN), lambda i, j: (i, j)),
        out_shape=jax.ShapeDtypeStruct((t, D_MAIN), BF16),
        compiler_params=_params(2),
        name="inproj",
    )(h, w_main)


POOL_TM = 1024
POOL_SUB = 256


def _pool_kernel(u_ref, up_ref, g_ref, w_ref, ps_ref, o_ref):
    grp = pl.program_id(0)
    i = pl.program_id(1)
    ts = POOL_SUB
    win = jnp.left_shift(2, grp)

    row = lax.broadcasted_iota(jnp.int32, (ts, ts), 0)
    col = lax.broadcasted_iota(jnp.int32, (ts, ts), 1)
    band = jnp.logical_and(col <= row, col > row - win).astype(BF16)
    hr = lax.broadcasted_iota(jnp.int32, (POOL_HALO, POOL_HALO), 0)
    hc = lax.broadcasted_iota(jnp.int32, (POOL_HALO, POOL_HALO), 1)
    hband = hr - hc + POOL_HALO < win
    pad = jnp.zeros((ts - POOL_HALO, POOL_GROUP), F32)
    w = w_ref[0]
    ps = ps_ref[...]

    for s in range(POOL_TM // ts):
        r0 = s * ts
        u = u_ref[r0:r0 + ts, :]
        if s == 0:
            halo = up_ref[...]
            hb = jnp.logical_and(hband, i > 0).astype(BF16)
        else:
            halo = u_ref[r0 - POOL_HALO:r0, :]
            hb = hband.astype(BF16)
        hsum = _dot(hb, halo)
        wsum = _dot(band, u) + jnp.concatenate([hsum, pad], axis=0)
        t_glob = i * POOL_TM + r0 + lax.broadcasted_iota(jnp.int32, (ts, 1), 0)
        cnt = jnp.minimum(t_glob + 1, win).astype(F32)
        pooled = wsum * (1.0 / cnt) - u.astype(F32)
        mixed = _dot(pooled.astype(BF16), w)
        y = mixed * ps * _silu(g_ref[r0:r0 + ts, :].astype(F32))
        o_ref[r0:r0 + ts, :] = y.astype(o_ref.dtype)


def _pool(z, w_pool_bf, pool_scale):
    t = z.shape[0]
    hb = POOL_TM // POOL_HALO
    gb = OFF_GP // POOL_GROUP
    return pl.pallas_call(
        _pool_kernel,
        grid=(len(POOL_WINDOWS), t // POOL_TM),
        in_specs=[
            pl.BlockSpec((POOL_TM, POOL_GROUP), lambda g, i: (i, g)),
            pl.BlockSpec((POOL_HALO, POOL_GROUP), lambda g, i: (jnp.maximum(i * hb - 1, 0), g)),
            pl.BlockSpec((POOL_TM, POOL_GROUP), lambda g, i: (i, gb + g)),
            pl.BlockSpec((1, POOL_GROUP, POOL_GROUP), lambda g, i: (g, 0, 0)),
            pl.BlockSpec((1, POOL_GROUP), lambda g, i: (0, g)),
        ],
        out_specs=pl.BlockSpec((POOL_TM, POOL_GROUP), lambda g, i: (i, g)),
        out_shape=jax.ShapeDtypeStruct((t, D_POOL), BF16),
        compiler_params=_params(2),
        name="pool",
    )(z, z, z, w_pool_bf, pool_scale)


GLA_TR = 256
GLA_LEVELS = GLA_TR.bit_length() - 1
GLA_MXU_LEVELS = 3


def _gla_constants():
    r = np.arange(GLA_TR)[:, None]
    t = np.arange(GLA_TR)[None, :]
    mats = []
    for k in range(GLA_MXU_LEVELS):
        half = ((r >> (k + 1)) << (k + 1)) + (1 << k)
        second = ((r >> k) & 1) == 1
        mats.append(np.where(second, (t >= half) & (t <= r), (t > r) & (t < half)))
    mats.append(t <= r)
    tcat = np.concatenate(mats, axis=0).astype(np.float32)
    diff = np.maximum(r ^ t, 1)
    level = np.where(t == r, GLA_LEVELS, np.where(t < r, np.floor(np.log2(diff)).astype(np.int64), -1))
    masks = np.stack([(level == k) for k in range(GLA_LEVELS + 1)]).astype(np.float32)
    return jnp.asarray(tcat, BF16), jnp.asarray(masks, BF16)


def _gla_head(q_bf, k_bf, v, gg, g2, nw, tcat, mk_ref, st_ref, hd):
    tr, nl = GLA_TR, GLA_LEVELS
    g_hi, g_lo = _split_bf16(g2)
    dall = _dot(tcat, g_hi) + _dot(tcat, g_lo)
    b2 = dall[GLA_MXU_LEVELS * tr:]
    kf = k_bf.astype(F32)
    qf = q_bf.astype(F32) * (GLA_DK ** -0.5)
    rbit = lax.broadcasted_iota(jnp.int32, (tr, 1), 0)

    a_bf = mk_ref[nl] * _dot_nt(qf.astype(BF16), k_bf).astype(BF16)
    for k in range(nl):
        if k < GLA_MXU_LEVELS:
            side = jnp.where(((rbit >> k) & 1) == 1, qf, kf)
            m = side * jnp.exp2(dall[k * tr:(k + 1) * tr])
        else:
            s = 1 << k
            parts = []
            for j in range(tr // (2 * s)):
                lo = 2 * s * j
                bp = b2[lo + s - 1:lo + s]
                parts.append(kf[lo:lo + s] * jnp.exp2(bp - b2[lo:lo + s]))
                parts.append(qf[lo + s:lo + 2 * s] * jnp.exp2(b2[lo + s:lo + 2 * s] - bp))
            m = jnp.concatenate(parts, axis=0)
        m = m.astype(BF16)
        a_bf = a_bf + mk_ref[k] * _dot_nt(m, m).astype(BF16)

    b_end = b2[tr - 1:tr]
    q_in = (qf * jnp.exp2(b2)).astype(BF16)
    k_out = (kf * jnp.exp2(b_end - b2)).astype(BF16)
    st = st_ref[hd]
    o = _dot(a_bf, v) + _dot_nt(q_in, st.astype(BF16))
    st_ref[hd] = jnp.exp2(b_end) * st + _dot_tn(v, k_out)

    ms = jnp.mean(o * o, axis=-1, keepdims=True)
    on = o * lax.rsqrt(ms + EPS) * nw
    return (on * _silu(gg.astype(F32))).astype(BF16)


def _gla_kernel(q_ref, k_ref, v_ref, gg_ref, a_ref, wa_ref, ba_ref, nw_ref, tc_ref, mk_ref,
                o_ref, st_ref):
    @pl.when(pl.program_id(0) == 0)
    def _():
        st_ref[...] = jnp.zeros_like(st_ref)

    a_hi, a_lo = _split_bf16(a_ref[...])
    w_hi, w_lo = _split_bf16(wa_ref[...])
    xg = _dot(a_hi, w_hi) + _dot(a_lo, w_hi) + _dot(a_hi, w_lo) + ba_ref[...]
    g = (jnp.minimum(xg, 0.0) - jnp.log1p(jnp.exp(-jnp.abs(xg)))) / GLA_TAU
    g2 = g * LOG2E
    tcat = tc_ref[...]
    nw = nw_ref[...]
    for hd in range(GLA_HEADS):
        ks = slice(hd * GLA_DK, (hd + 1) * GLA_DK)
        vs = slice(hd * GLA_DV, (hd + 1) * GLA_DV)
        o_ref[:, vs] = _gla_head(q_ref[:, ks], k_ref[:, ks], v_ref[:, vs], gg_ref[:, vs],
                                 g2[:, ks], nw, tcat, mk_ref, st_ref, hd)


def _gla(z, a_lr, w_alpha_pad, b_alpha, gla_norm_w):
    t = z.shape[0]
    tcat, masks = _gla_constants()
    return pl.pallas_call(
        _gla_kernel,
        grid=(t // GLA_TR,),
        in_specs=[
            pl.BlockSpec((GLA_TR, GLA_KEY), lambda i: (i, OFF_Q // GLA_KEY)),
            pl.BlockSpec((GLA_TR, GLA_KEY), lambda i: (i, OFF_K // GLA_KEY)),
            pl.BlockSpec((GLA_TR, D_GLA), lambda i: (i, OFF_V // D_GLA)),
            pl.BlockSpec((GLA_TR, D_GLA), lambda i: (i, OFF_GG // D_GLA)),
            pl.BlockSpec((GLA_TR, RANK_PAD), lambda i: (i, 0)),
            pl.BlockSpec((RANK_PAD, GLA_KEY), lambda i: (0, 0)),
            pl.BlockSpec((1, GLA_KEY), lambda i: (0, 0)),
            pl.BlockSpec((1, GLA_DV), lambda i: (0, 0)),
            pl.BlockSpec(tcat.shape, lambda i: (0, 0)),
            pl.BlockSpec(masks.shape, lambda i: (0, 0, 0)),
        ],
        out_specs=pl.BlockSpec((GLA_TR, D_GLA), lambda i: (i, 0)),
        out_shape=jax.ShapeDtypeStruct((t, D_GLA), BF16),
        scratch_shapes=[pltpu.VMEM((GLA_HEADS, GLA_DV, GLA_DK), F32)],
        compiler_params=_params(1),
        name="gla",
    )(z, z, z, z, a_lr, w_alpha_pad, b_alpha, gla_norm_w, tcat, masks)


OUT_TM = 256
OUT_NC = 512
OUT_VMEM_LIMIT = 60 * 1024 * 1024


def _outproj_kernel(yp_ref, yg_ref, w_ref, x_ref, gate_ref, fw_ref, o_ref):
    yp = yp_ref[...]
    yg = yg_ref[...]
    ssq = jnp.zeros((OUT_TM, 1), F32)
    for c in range(D_MODEL // OUT_NC):
        cs = slice(c * OUT_NC, (c + 1) * OUT_NC)
        y = _dot(yp, w_ref[:D_POOL, cs]) + _dot(yg, w_ref[D_POOL:, cs])
        xn = x_ref[:, cs] + gate_ref[:, cs] * y
        o_ref[:, cs] = xn
        ssq = ssq + jnp.sum(xn * xn, axis=-1, keepdims=True)
    inv = lax.rsqrt(ssq * (1.0 / D_MODEL) + EPS)
    o_ref[...] = o_ref[...] * inv * fw_ref[...]


def _outproj(y_pool, y_gla, w_out_bf, x2, mod, final_norm_w):
    t = x2.shape[0]
    return pl.pallas_call(
        _outproj_kernel,
        grid=(t // OUT_TM,),
        in_specs=[
            pl.BlockSpec((OUT_TM, D_POOL), lambda i: (i, 0)),
            pl.BlockSpec((OUT_TM, D_GLA), lambda i: (i, 0)),
            pl.BlockSpec((D_MODEL, D_MODEL), lambda i: (0, 0), pipeline_mode=pl.Buffered(1)),
            pl.BlockSpec((OUT_TM, D_MODEL), lambda i: (i, 0)),
            pl.BlockSpec((1, D_MODEL), lambda i: (0, 2)),
            pl.BlockSpec((1, D_MODEL), lambda i: (0, 0)),
        ],
        out_specs=pl.BlockSpec((OUT_TM, D_MODEL), lambda i: (i, 0)),
        out_shape=jax.ShapeDtypeStruct((t, D_MODEL), F32),
        compiler_params=_params(1, OUT_VMEM_LIMIT),
        name="outproj",
    )(y_pool, y_gla, w_out_bf, x2, mod, final_norm_w)


def kernel(x, c, w_ada, b_ada, norm_w, w_in, w_pool, pool_scale, w_alpha, b_alpha,
           gla_norm_w, w_out, final_norm_w):
    bsz, t, d = x.shape
    assert bsz == 1 and d == D_MODEL and w_ada.shape[0] == 1
    x2 = x.reshape(t, d)

    mod = _adaln(c.reshape(d, 1), w_ada[0], b_ada)

    w_main = _castw(w_in[0])
    w_a = jnp.pad(w_in[0, :, D_MAIN:], ((0, 0), (0, RANK_PAD - GLA_RANK))).astype(BF16)
    h, a_lr = _norm(x2, norm_w, mod, w_a)
    z = _inproj(h, w_main)

    y_pool = _pool(z, w_pool[0].astype(BF16), pool_scale)

    w_alpha_pad = jnp.pad(w_alpha[0], ((0, RANK_PAD - GLA_RANK), (0, 0)))
    y_gla = _gla(z, a_lr, w_alpha_pad, b_alpha, gla_norm_w)

    out = _outproj(y_pool, y_gla, w_out[0].astype(BF16), x2, mod, final_norm_w.reshape(1, d))
    return out.reshape(bsz, t, d).astype(x.dtype)
```

```python
import math

import jax
import jax.numpy as jnp
import numpy as np
from jax import lax
from jax.experimental import pallas as pl
from jax.experimental.pallas import tpu as pltpu

F32 = jnp.float32
BF16 = jnp.bfloat16

D_MODEL = 4096
D_POOL = 2048
POOL_WINDOWS = (2, 4, 8, 16)
POOL_GROUP = 512
POOL_HALO = 16
D_GLA = 2048
GLA_HEADS = 4
GLA_DV = 512
GLA_DK = 256
GLA_KEY = 1024
GLA_RANK = 16
GLA_TAU = 16.0
EPS = 1e-6
D_MAIN = 2 * D_POOL + 2 * GLA_KEY + 2 * D_GLA
RANK_PAD = 128
LOG2E = math.log2(math.e)

OFF_U, OFF_GP, OFF_Q, OFF_K, OFF_V, OFF_GG = 0, 2048, 4096, 5120, 6144, 8192

VMEM_LIMIT = 56 * 1024 * 1024


def _dot(a, b):
    return jnp.dot(a, b, preferred_element_type=F32)


def _dot_nt(a, b):
    return lax.dot_general(a, b, (((1,), (1,)), ((), ())), preferred_element_type=F32)


def _dot_tn(a, b):
    return lax.dot_general(a, b, (((0,), (0,)), ((), ())), preferred_element_type=F32)


def _split_bf16(a):
    hi = a.astype(BF16)
    lo = (a - hi.astype(F32)).astype(BF16)
    return hi, lo


def _silu(v):
    return v * jax.nn.sigmoid(v)


def _params(n_axes, vmem_limit=VMEM_LIMIT):
    return pltpu.CompilerParams(
        dimension_semantics=("arbitrary",) * n_axes, vmem_limit_bytes=vmem_limit)


ADA_TN = 512
ADA_KC = 256


def _adaln_kernel(c_ref, w_ref, b_ref, o_ref):
    tn = o_ref.shape[1]

    def body(kk, acc):
        r = pl.multiple_of(kk * ADA_KC, ADA_KC)
        cc = c_ref[pl.ds(r, ADA_KC), :]
        p = _silu(cc) * w_ref[pl.ds(r, ADA_KC), :]
        return acc + p.reshape(ADA_KC // 8, 8, tn).sum(axis=0)

    acc = lax.fori_loop(0, D_MODEL // ADA_KC, body, jnp.zeros((8, tn), F32))
    o_ref[...] = acc.sum(axis=0, keepdims=True) + b_ref[...]


def _adaln(c_col, w_ada, b_ada):
    n = w_ada.shape[1]
    return pl.pallas_call(
        _adaln_kernel,
        grid=(n // ADA_TN,),
        in_specs=[
            pl.BlockSpec((D_MODEL, 1), lambda j: (0, 0)),
            pl.BlockSpec((D_MODEL, ADA_TN), lambda j: (0, j)),
            pl.BlockSpec((1, ADA_TN), lambda j: (0, j)),
        ],
        out_specs=pl.BlockSpec((1, ADA_TN), lambda j: (0, j)),
        out_shape=jax.ShapeDtypeStruct((1, n), F32),
        compiler_params=_params(1),
        name="adaln",
    )(c_col, w_ada, b_ada)


CAST_TM = 512


def _castw_kernel(w_ref, o_ref):
    o_ref[...] = w_ref[...].astype(o_ref.dtype)


def _castw(w_t):
    cols = w_t.shape[1]
    return pl.pallas_call(
        _castw_kernel,
        grid=(D_MAIN // CAST_TM,),
        in_specs=[pl.BlockSpec((CAST_TM, cols), lambda j: (j, 0))],
        out_specs=pl.BlockSpec((CAST_TM, cols), lambda j: (j, 0)),
        out_shape=jax.ShapeDtypeStruct((D_MAIN, cols), BF16),
        compiler_params=_params(1),
        name="castw",
    )(w_t)


NORM_TM = 512
NORM_RC = 64


def _norm_kernel(x_ref, nw_ref, shift_ref, scale_ref, wa_ref, h_ref, a_ref):
    mul = nw_ref[...] * (1.0 + scale_ref[...])
    sh = shift_ref[...]

    def body(r, carry):
        r0 = pl.multiple_of(r * NORM_RC, NORM_RC)
        xs = x_ref[pl.ds(r0, NORM_RC), :]
        ms = jnp.mean(xs * xs, axis=-1, keepdims=True)
        h_ref[pl.ds(r0, NORM_RC), :] = (xs * lax.rsqrt(ms + EPS) * mul + sh).astype(BF16)
        return carry

    lax.fori_loop(0, NORM_TM // NORM_RC, body, 0)
    a_ref[...] = _dot_nt(h_ref[...], wa_ref[...])


def _norm(x2, norm_w, mod, w_a):
    t = x2.shape[0]
    return pl.pallas_call(
        _norm_kernel,
        grid=(t // NORM_TM,),
        in_specs=[
            pl.BlockSpec((NORM_TM, D_MODEL), lambda i: (i, 0)),
            pl.BlockSpec((1, D_MODEL), lambda i: (0, 0)),
            pl.BlockSpec((1, D_MODEL), lambda i: (0, 0)),
            pl.BlockSpec((1, D_MODEL), lambda i: (0, 1)),
            pl.BlockSpec((RANK_PAD, D_MODEL), lambda i: (0, 0)),
        ],
        out_specs=[
            pl.BlockSpec((NORM_TM, D_MODEL), lambda i: (i, 0)),
            pl.BlockSpec((NORM_TM, RANK_PAD), lambda i: (i, 0)),
        ],
        out_shape=[
            jax.ShapeDtypeStruct((t, D_MODEL), BF16),
            jax.ShapeDtypeStruct((t, RANK_PAD), F32),
        ],
        compiler_params=_params(1),
        name="norm",
    )(x2, norm_w, mod, mod, w_a)


IN_TM = 1024
IN_TN = 1024


IN_WO_ROWS = 64


def _inproj_kernel(h_ref, w_ref, wo_ref, z_ref, wo_bf_ref):
    z_ref[...] = _dot_nt(h_ref[...], w_ref[...]).astype(z_ref.dtype)
    wo_bf_ref[...] = wo_ref[...].astype(wo_bf_ref.dtype)


def _inproj(h, w_main, w_out):
    t = h.shape[0]
    n_j = D_MAIN // IN_TN
    n_slabs = w_out.shape[0] // IN_WO_ROWS
    assert (t // IN_TM) * n_j >= n_slabs

    def slab(i, j):
        return (jnp.minimum(i * n_j + j, n_slabs - 1), 0)

    return pl.pallas_call(
        _inproj_kernel,
        grid=(t // IN_TM, n_j),
        in_specs=[
            pl.BlockSpec((IN_TM, D_MODEL), lambda i, j: (i, 0)),
            pl.BlockSpec((IN_TN, D_MODEL), lambda i, j: (j, 0)),
            pl.BlockSpec((IN_WO_ROWS, w_out.shape[1]), slab),
        ],
        out_specs=[
            pl.BlockSpec((IN_TM, IN_TN), lambda i, j: (i, j)),
            pl.BlockSpec((IN_WO_ROWS, w_out.shape[1]), slab),
        ],
        out_shape=[
            jax.ShapeDtypeStruct((t, D_MAIN), BF16),
            jax.ShapeDtypeStruct(w_out.shape, BF16),
        ],
        compiler_params=_params(2),
        name="inproj",
    )(h, w_main, w_out)


POOL_TM = 1024
POOL_SUB = 256


def _pool_kernel(u_ref, up_ref, g_ref, w_ref, ps_ref, o_ref):
    grp = pl.program_id(0)
    i = pl.program_id(1)
    ts = POOL_SUB
    win = jnp.left_shift(2, grp)

    row = lax.broadcasted_iota(jnp.int32, (ts, ts), 0)
    col = lax.broadcasted_iota(jnp.int32, (ts, ts), 1)
    band = jnp.logical_and(col <= row, col > row - win).astype(BF16)
    hr = lax.broadcasted_iota(jnp.int32, (POOL_HALO, POOL_HALO), 0)
    hc = lax.broadcasted_iota(jnp.int32, (POOL_HALO, POOL_HALO), 1)
    hband = hr - hc + POOL_HALO < win
    pad = jnp.zeros((ts - POOL_HALO, POOL_GROUP), F32)
    w = w_ref[0]
    ps = ps_ref[...]

    for s in range(POOL_TM // ts):
        r0 = s * ts
        u = u_ref[r0:r0 + ts, :]
        if s == 0:
            halo = up_ref[...]
            hb = jnp.logical_and(hband, i > 0).astype(BF16)
        else:
            halo = u_ref[r0 - POOL_HALO:r0, :]
            hb = hband.astype(BF16)
        hsum = _dot(hb, halo)
        wsum = _dot(band, u) + jnp.concatenate([hsum, pad], axis=0)
        t_glob = i * POOL_TM + r0 + lax.broadcasted_iota(jnp.int32, (ts, 1), 0)
        cnt = jnp.minimum(t_glob + 1, win).astype(F32)
        pooled = wsum * (1.0 / cnt) - u.astype(F32)
        mixed = _dot(pooled.astype(BF16), w)
        y = mixed * ps * _silu(g_ref[r0:r0 + ts, :].astype(F32))
        o_ref[r0:r0 + ts, :] = y.astype(o_ref.dtype)


def _pool(z, w_pool_bf, pool_scale):
    t = z.shape[0]
    hb = POOL_TM // POOL_HALO
    gb = OFF_GP // POOL_GROUP
    return pl.pallas_call(
        _pool_kernel,
        grid=(len(POOL_WINDOWS), t // POOL_TM),
        in_specs=[
            pl.BlockSpec((POOL_TM, POOL_GROUP), lambda g, i: (i, g)),
            pl.BlockSpec((POOL_HALO, POOL_GROUP), lambda g, i: (jnp.maximum(i * hb - 1, 0), g)),
            pl.BlockSpec((POOL_TM, POOL_GROUP), lambda g, i: (i, gb + g)),
            pl.BlockSpec((1, POOL_GROUP, POOL_GROUP), lambda g, i: (g, 0, 0)),
            pl.BlockSpec((1, POOL_GROUP), lambda g, i: (0, g)),
        ],
        out_specs=pl.BlockSpec((POOL_TM, POOL_GROUP), lambda g, i: (i, g)),
        out_shape=jax.ShapeDtypeStruct((t, D_POOL), BF16),
        compiler_params=_params(2),
        name="pool",
    )(z, z, z, w_pool_bf, pool_scale)


GLA_TR = 256
GLA_LEVELS = GLA_TR.bit_length() - 1
GLA_MXU_LEVELS = 3


def _gla_constants():
    r = np.arange(GLA_TR)[:, None]
    t = np.arange(GLA_TR)[None, :]
    mats = []
    for k in range(GLA_MXU_LEVELS):
        half = ((r >> (k + 1)) << (k + 1)) + (1 << k)
        second = ((r >> k) & 1) == 1
        mats.append(np.where(second, (t >= half) & (t <= r), (t > r) & (t < half)))
    mats.append(t <= r)
    tcat = np.concatenate(mats, axis=0).astype(np.float32)
    diff = np.maximum(r ^ t, 1)
    level = np.where(t == r, GLA_LEVELS, np.where(t < r, np.floor(np.log2(diff)).astype(np.int64), -1))
    masks = np.stack([(level == k) for k in range(GLA_LEVELS + 1)]).astype(np.float32)
    return jnp.asarray(tcat, BF16), jnp.asarray(masks, BF16)


def _gla_head(q_bf, k_bf, v, gg, g2, nw, tcat, mk_ref, st_ref, hd):
    tr, nl = GLA_TR, GLA_LEVELS
    g_hi, g_lo = _split_bf16(g2)
    dall = _dot(tcat, g_hi) + _dot(tcat, g_lo)
    b2 = dall[GLA_MXU_LEVELS * tr:]
    kf = k_bf.astype(F32)
    qf = q_bf.astype(F32) * (GLA_DK ** -0.5)
    rbit = lax.broadcasted_iota(jnp.int32, (tr, 1), 0)

    a_bf = mk_ref[nl] * _dot_nt(qf.astype(BF16), k_bf).astype(BF16)
    for k in range(nl):
        if k < GLA_MXU_LEVELS:
            side = jnp.where(((rbit >> k) & 1) == 1, qf, kf)
            m = side * jnp.exp2(dall[k * tr:(k + 1) * tr])
        else:
            s = 1 << k
            parts = []
            for j in range(tr // (2 * s)):
                lo = 2 * s * j
                bp = b2[lo + s - 1:lo + s]
                parts.append(kf[lo:lo + s] * jnp.exp2(bp - b2[lo:lo + s]))
                parts.append(qf[lo + s:lo + 2 * s] * jnp.exp2(b2[lo + s:lo + 2 * s] - bp))
            m = jnp.concatenate(parts, axis=0)
        m = m.astype(BF16)
        a_bf = a_bf + mk_ref[k] * _dot_nt(m, m).astype(BF16)

    b_end = b2[tr - 1:tr]
    q_in = (qf * jnp.exp2(b2)).astype(BF16)
    k_out = (kf * jnp.exp2(b_end - b2)).astype(BF16)
    st = st_ref[hd]
    o = _dot(a_bf, v) + _dot_nt(q_in, st.astype(BF16))
    st_ref[hd] = jnp.exp2(b_end) * st + _dot_tn(v, k_out)

    ms = jnp.mean(o * o, axis=-1, keepdims=True)
    on = o * lax.rsqrt(ms + EPS) * nw
    return (on * _silu(gg.astype(F32))).astype(BF16)


def _gla_kernel(q_ref, k_ref, v_ref, gg_ref, a_ref, wa_ref, ba_ref, nw_ref, tc_ref, mk_ref,
                o_ref, st_ref):
    @pl.when(pl.program_id(0) == 0)
    def _():
        st_ref[...] = jnp.zeros_like(st_ref)

    a_hi, a_lo = _split_bf16(a_ref[...])
    w_hi, w_lo = _split_bf16(wa_ref[...])
    xg = _dot(a_hi, w_hi) + _dot(a_lo, w_hi) + _dot(a_hi, w_lo) + ba_ref[...]
    g = (jnp.minimum(xg, 0.0) - jnp.log1p(jnp.exp(-jnp.abs(xg)))) / GLA_TAU
    g2 = g * LOG2E
    tcat = tc_ref[...]
    nw = nw_ref[...]
    for hd in range(GLA_HEADS):
        ks = slice(hd * GLA_DK, (hd + 1) * GLA_DK)
        vs = slice(hd * GLA_DV, (hd + 1) * GLA_DV)
        o_ref[:, vs] = _gla_head(q_ref[:, ks], k_ref[:, ks], v_ref[:, vs], gg_ref[:, vs],
                                 g2[:, ks], nw, tcat, mk_ref, st_ref, hd)


def _gla(z, a_lr, w_alpha_pad, b_alpha, gla_norm_w):
    t = z.shape[0]
    tcat, masks = _gla_constants()
    return pl.pallas_call(
        _gla_kernel,
        grid=(t // GLA_TR,),
        in_specs=[
            pl.BlockSpec((GLA_TR, GLA_KEY), lambda i: (i, OFF_Q // GLA_KEY)),
            pl.BlockSpec((GLA_TR, GLA_KEY), lambda i: (i, OFF_K // GLA_KEY)),
            pl.BlockSpec((GLA_TR, D_GLA), lambda i: (i, OFF_V // D_GLA)),
            pl.BlockSpec((GLA_TR, D_GLA), lambda i: (i, OFF_GG // D_GLA)),
            pl.BlockSpec((GLA_TR, RANK_PAD), lambda i: (i, 0)),
            pl.BlockSpec((RANK_PAD, GLA_KEY), lambda i: (0, 0)),
            pl.BlockSpec((1, GLA_KEY), lambda i: (0, 0)),
            pl.BlockSpec((1, GLA_DV), lambda i: (0, 0)),
            pl.BlockSpec(tcat.shape, lambda i: (0, 0)),
            pl.BlockSpec(masks.shape, lambda i: (0, 0, 0)),
        ],
        out_specs=pl.BlockSpec((GLA_TR, D_GLA), lambda i: (i, 0)),
        out_shape=jax.ShapeDtypeStruct((t, D_GLA), BF16),
        scratch_shapes=[pltpu.VMEM((GLA_HEADS, GLA_DV, GLA_DK), F32)],
        compiler_params=_params(1),
        name="gla",
    )(z, z, z, z, a_lr, w_alpha_pad, b_alpha, gla_norm_w, tcat, masks)


OUT_TM = 256
OUT_NC = 512
OUT_VMEM_LIMIT = 60 * 1024 * 1024


def _outproj_kernel(yp_ref, yg_ref, w_ref, x_ref, gate_ref, fw_ref, o_ref):
    yp = yp_ref[...]
    yg = yg_ref[...]
    ssq = jnp.zeros((OUT_TM, 1), F32)
    for c in range(D_MODEL // OUT_NC):
        cs = slice(c * OUT_NC, (c + 1) * OUT_NC)
        y = _dot(yp, w_ref[:D_POOL, cs]) + _dot(yg, w_ref[D_POOL:, cs])
        xn = x_ref[:, cs] + gate_ref[:, cs] * y
        o_ref[:, cs] = xn
        ssq = ssq + jnp.sum(xn * xn, axis=-1, keepdims=True)
    inv = lax.rsqrt(ssq * (1.0 / D_MODEL) + EPS)
    o_ref[...] = o_ref[...] * inv * fw_ref[...]


def _outproj(y_pool, y_gla, w_out_bf, x2, mod, final_norm_w):
    t = x2.shape[0]
    return pl.pallas_call(
        _outproj_kernel,
        grid=(t // OUT_TM,),
        in_specs=[
            pl.BlockSpec((OUT_TM, D_POOL), lambda i: (i, 0)),
            pl.BlockSpec((OUT_TM, D_GLA), lambda i: (i, 0)),
            pl.BlockSpec((D_MODEL, D_MODEL), lambda i: (0, 0), pipeline_mode=pl.Buffered(1)),
            pl.BlockSpec((OUT_TM, D_MODEL), lambda i: (i, 0)),
            pl.BlockSpec((1, D_MODEL), lambda i: (0, 2)),
            pl.BlockSpec((1, D_MODEL), lambda i: (0, 0)),
        ],
        out_specs=pl.BlockSpec((OUT_TM, D_MODEL), lambda i: (i, 0)),
        out_shape=jax.ShapeDtypeStruct((t, D_MODEL), F32),
        compiler_params=_params(1, OUT_VMEM_LIMIT),
        name="outproj",
    )(y_pool, y_gla, w_out_bf, x2, mod, final_norm_w)


def kernel(x, c, w_ada, b_ada, norm_w, w_in, w_pool, pool_scale, w_alpha, b_alpha,
           gla_norm_w, w_out, final_norm_w):
    bsz, t, d = x.shape
    assert bsz == 1 and d == D_MODEL and w_ada.shape[0] == 1
    x2 = x.reshape(t, d)

    mod = _adaln(c.reshape(d, 1), w_ada[0], b_ada)

    w_in_t = w_in[0].T
    w_main = _castw(w_in_t)
    w_a = jnp.pad(w_in_t[D_MAIN:], ((0, RANK_PAD - GLA_RANK), (0, 0))).astype(BF16)
    h, a_lr = _norm(x2, norm_w, mod, w_a)
    z, w_out_bf = _inproj(h, w_main, w_out[0])

    y_pool = _pool(z, w_pool[0].astype(BF16), pool_scale)

    w_alpha_pad = jnp.pad(w_alpha[0], ((0, RANK_PAD - GLA_RANK), (0, 0)))
    y_gla = _gla(z, a_lr, w_alpha_pad, b_alpha, gla_norm_w)

    out = _outproj(y_pool, y_gla, w_out_bf, x2, mod, final_norm_w.reshape(1, d))
    return out.reshape(bsz, t, d).astype(x.dtype)
```

```python
import math

import jax
import jax.numpy as jnp
import numpy as np
from jax import lax
from jax.experimental import pallas as pl
from jax.experimental.pallas import tpu as pltpu

F32 = jnp.float32
BF16 = jnp.bfloat16

D_MODEL = 4096
D_POOL = 2048
POOL_WINDOWS = (2, 4, 8, 16)
POOL_GROUP = 512
POOL_HALO = 16
D_GLA = 2048
GLA_HEADS = 4
GLA_DV = 512
GLA_DK = 256
GLA_KEY = 1024
GLA_RANK = 16
GLA_TAU = 16.0
EPS = 1e-6
D_MAIN = 2 * D_POOL + 2 * GLA_KEY + 2 * D_GLA
RANK_PAD = 128
LOG2E = math.log2(math.e)

OFF_U, OFF_GP, OFF_Q, OFF_K, OFF_V, OFF_GG = 0, 2048, 4096, 5120, 6144, 8192

VMEM_LIMIT = 56 * 1024 * 1024


def _dot(a, b):
    return jnp.dot(a, b, preferred_element_type=F32)


def _dot_nt(a, b):
    return lax.dot_general(a, b, (((1,), (1,)), ((), ())), preferred_element_type=F32)


def _dot_tn(a, b):
    return lax.dot_general(a, b, (((0,), (0,)), ((), ())), preferred_element_type=F32)


def _split_bf16(a):
    hi = a.astype(BF16)
    lo = (a - hi.astype(F32)).astype(BF16)
    return hi, lo


def _silu(v):
    return v * jax.nn.sigmoid(v)


def _params(n_axes, vmem_limit=VMEM_LIMIT):
    return pltpu.CompilerParams(
        dimension_semantics=("arbitrary",) * n_axes, vmem_limit_bytes=vmem_limit)


ADA_TN = 1024
ADA_KC = 256


def _adaln_kernel(c_ref, w_ref, b_ref, o_ref):
    tn = o_ref.shape[1]

    def body(kk, acc):
        r = pl.multiple_of(kk * ADA_KC, ADA_KC)
        cc = c_ref[pl.ds(r, ADA_KC), :]
        p = _silu(cc) * w_ref[pl.ds(r, ADA_KC), :]
        return acc + p.reshape(ADA_KC // 8, 8, tn).sum(axis=0)

    acc = lax.fori_loop(0, D_MODEL // ADA_KC, body, jnp.zeros((8, tn), F32))
    o_ref[...] = acc.sum(axis=0, keepdims=True) + b_ref[...]


def _adaln(c_col, w_ada, b_ada):
    n = w_ada.shape[1]
    return pl.pallas_call(
        _adaln_kernel,
        grid=(n // ADA_TN,),
        in_specs=[
            pl.BlockSpec((D_MODEL, 1), lambda j: (0, 0)),
            pl.BlockSpec((D_MODEL, ADA_TN), lambda j: (0, j)),
            pl.BlockSpec((1, ADA_TN), lambda j: (0, j)),
        ],
        out_specs=pl.BlockSpec((1, ADA_TN), lambda j: (0, j)),
        out_shape=jax.ShapeDtypeStruct((1, n), F32),
        compiler_params=_params(1),
        name="adaln",
    )(c_col, w_ada, b_ada)


CAST_TM = 512


def _castw_kernel(w_ref, o_ref):
    o_ref[...] = w_ref[...].astype(o_ref.dtype)


def _castw(w_t):
    cols = w_t.shape[1]
    return pl.pallas_call(
        _castw_kernel,
        grid=(D_MAIN // CAST_TM,),
        in_specs=[pl.BlockSpec((CAST_TM, cols), lambda j: (j, 0))],
        out_specs=pl.BlockSpec((CAST_TM, cols), lambda j: (j, 0)),
        out_shape=jax.ShapeDtypeStruct((D_MAIN, cols), BF16),
        compiler_params=_params(1),
        name="castw",
    )(w_t)


NORM_TM = 512
NORM_RC = 64


def _norm_kernel(x_ref, nw_ref, shift_ref, scale_ref, wa_ref, h_ref, a_ref):
    mul = nw_ref[...] * (1.0 + scale_ref[...])
    sh = shift_ref[...]

    def body(r, carry):
        r0 = pl.multiple_of(r * NORM_RC, NORM_RC)
        xs = x_ref[pl.ds(r0, NORM_RC), :]
        ms = jnp.mean(xs * xs, axis=-1, keepdims=True)
        h_ref[pl.ds(r0, NORM_RC), :] = (xs * lax.rsqrt(ms + EPS) * mul + sh).astype(BF16)
        return carry

    lax.fori_loop(0, NORM_TM // NORM_RC, body, 0)
    a_ref[...] = _dot_nt(h_ref[...], wa_ref[...])


def _norm(x2, norm_w, mod, w_a):
    t = x2.shape[0]
    return pl.pallas_call(
        _norm_kernel,
        grid=(t // NORM_TM,),
        in_specs=[
            pl.BlockSpec((NORM_TM, D_MODEL), lambda i: (i, 0)),
            pl.BlockSpec((1, D_MODEL), lambda i: (0, 0)),
            pl.BlockSpec((1, D_MODEL), lambda i: (0, 0)),
            pl.BlockSpec((1, D_MODEL), lambda i: (0, 1)),
            pl.BlockSpec((RANK_PAD, D_MODEL), lambda i: (0, 0)),
        ],
        out_specs=[
            pl.BlockSpec((NORM_TM, D_MODEL), lambda i: (i, 0)),
            pl.BlockSpec((NORM_TM, RANK_PAD), lambda i: (i, 0)),
        ],
        out_shape=[
            jax.ShapeDtypeStruct((t, D_MODEL), BF16),
            jax.ShapeDtypeStruct((t, RANK_PAD), F32),
        ],
        compiler_params=_params(1),
        name="norm",
    )(x2, norm_w, mod, mod, w_a)


IN_TM = 1024
IN_TN = 1024


IN_WO_ROWS = 64


def _inproj_kernel(h_ref, w_ref, wo_ref, z_ref, wo_bf_ref):
    z_ref[...] = _dot_nt(h_ref[...], w_ref[...]).astype(z_ref.dtype)
    wo_bf_ref[...] = wo_ref[...].astype(wo_bf_ref.dtype)


def _inproj(h, w_main, w_out):
    t = h.shape[0]
    n_j = D_MAIN // IN_TN
    n_slabs = w_out.shape[0] // IN_WO_ROWS
    assert (t // IN_TM) * n_j >= n_slabs

    def slab(i, j):
        return (jnp.minimum(i * n_j + j, n_slabs - 1), 0)

    return pl.pallas_call(
        _inproj_kernel,
        grid=(t // IN_TM, n_j),
        in_specs=[
            pl.BlockSpec((IN_TM, D_MODEL), lambda i, j: (i, 0)),
            pl.BlockSpec((IN_TN, D_MODEL), lambda i, j: (j, 0)),
            pl.BlockSpec((IN_WO_ROWS, w_out.shape[1]), slab),
        ],
        out_specs=[
            pl.BlockSpec((IN_TM, IN_TN), lambda i, j: (i, j)),
            pl.BlockSpec((IN_WO_ROWS, w_out.shape[1]), slab),
        ],
        out_shape=[
            jax.ShapeDtypeStruct((t, D_MAIN), BF16),
            jax.ShapeDtypeStruct(w_out.shape, BF16),
        ],
        compiler_params=_params(2),
        name="inproj",
    )(h, w_main, w_out)


POOL_TM = 1024
POOL_SUB = 256


def _pool_kernel(u_ref, up_ref, g_ref, w_ref, ps_ref, o_ref):
    grp = pl.program_id(0)
    i = pl.program_id(1)
    ts = POOL_SUB
    win = jnp.left_shift(2, grp)

    row = lax.broadcasted_iota(jnp.int32, (ts, ts), 0)
    col = lax.broadcasted_iota(jnp.int32, (ts, ts), 1)
    in_win = jnp.logical_and(col <= row, col > row - win)
    eye = (row == col).astype(F32)
    hr = lax.broadcasted_iota(jnp.int32, (POOL_HALO, POOL_HALO), 0)
    hc = lax.broadcasted_iota(jnp.int32, (POOL_HALO, POOL_HALO), 1)
    in_halo = hr - hc + POOL_HALO < win
    rows = lax.broadcasted_iota(jnp.int32, (ts, 1), 0)

    def mixing(t0, has_history):
        inv = 1.0 / jnp.minimum(t0 + rows + 1, win).astype(F32)
        band = (jnp.where(in_win, inv, 0.0) - eye).astype(BF16)
        hmask = jnp.logical_and(in_halo, has_history)
        return band, jnp.where(hmask, inv[:POOL_HALO], 0.0).astype(BF16)

    first = mixing(i * POOL_TM, i > 0)
    rest = mixing(i * POOL_TM + ts, True)
    pad = jnp.zeros((ts - POOL_HALO, POOL_GROUP), F32)
    w = w_ref[0]
    ps = ps_ref[...]

    for s in range(POOL_TM // ts):
        r0 = s * ts
        band, hband = first if s == 0 else rest
        halo = up_ref[...] if s == 0 else u_ref[r0 - POOL_HALO:r0, :]
        pooled = _dot(band, u_ref[r0:r0 + ts, :]) + jnp.concatenate([_dot(hband, halo), pad], axis=0)
        mixed = _dot(pooled.astype(BF16), w)
        o_ref[r0:r0 + ts, :] = (mixed * ps).astype(BF16) * _silu(g_ref[r0:r0 + ts, :])


def _pool(z, w_pool_bf, pool_scale):
    t = z.shape[0]
    hb = POOL_TM // POOL_HALO
    gb = OFF_GP // POOL_GROUP
    return pl.pallas_call(
        _pool_kernel,
        grid=(len(POOL_WINDOWS), t // POOL_TM),
        in_specs=[
            pl.BlockSpec((POOL_TM, POOL_GROUP), lambda g, i: (i, g)),
            pl.BlockSpec((POOL_HALO, POOL_GROUP), lambda g, i: (jnp.maximum(i * hb - 1, 0), g)),
            pl.BlockSpec((POOL_TM, POOL_GROUP), lambda g, i: (i, gb + g)),
            pl.BlockSpec((1, POOL_GROUP, POOL_GROUP), lambda g, i: (g, 0, 0)),
            pl.BlockSpec((1, POOL_GROUP), lambda g, i: (0, g)),
        ],
        out_specs=pl.BlockSpec((POOL_TM, POOL_GROUP), lambda g, i: (i, g)),
        out_shape=jax.ShapeDtypeStruct((t, D_POOL), BF16),
        compiler_params=_params(2),
        name="pool",
    )(z, z, z, w_pool_bf, pool_scale)


GLA_TR = 256
GLA_LEVELS = GLA_TR.bit_length() - 1
GLA_MXU_LEVELS = 3


def _gla_constants():
    r = np.arange(GLA_TR)[:, None]
    t = np.arange(GLA_TR)[None, :]
    mats = []
    for k in range(GLA_MXU_LEVELS):
        half = ((r >> (k + 1)) << (k + 1)) + (1 << k)
        second = ((r >> k) & 1) == 1
        mats.append(np.where(second, (t >= half) & (t <= r), (t > r) & (t < half)))
    mats.append(t <= r)
    tcat = np.concatenate(mats, axis=0).astype(np.float32)
    diff = np.maximum(r ^ t, 1)
    level = np.where(t == r, GLA_LEVELS, np.where(t < r, np.floor(np.log2(diff)).astype(np.int64), -1))
    masks = np.stack([(level == k) for k in range(GLA_LEVELS + 1)]).astype(np.float32)
    return jnp.asarray(tcat, BF16), jnp.asarray(masks, BF16)


def _gla_head(q_bf, k_bf, v, gg, g2, nw, tcat, mk_ref, st_ref, hd):
    tr, nl = GLA_TR, GLA_LEVELS
    g_hi, g_lo = _split_bf16(g2)
    n_small = GLA_MXU_LEVELS * tr
    dall = _dot(tcat[:n_small], g_hi)
    b2 = _dot(tcat[n_small:], g_hi) + _dot(tcat[n_small:], g_lo)
    kf = k_bf.astype(F32)
    qf = q_bf.astype(F32) * (GLA_DK ** -0.5)
    rbit = lax.broadcasted_iota(jnp.int32, (tr, 1), 0)

    a_bf = mk_ref[nl] * _dot_nt(qf.astype(BF16), k_bf).astype(BF16)
    for k in range(nl):
        if k < GLA_MXU_LEVELS:
            side = jnp.where(((rbit >> k) & 1) == 1, qf, kf)
            m = side * jnp.exp2(dall[k * tr:(k + 1) * tr])
        else:
            s = 1 << k
            parts = []
            for j in range(tr // (2 * s)):
                lo = 2 * s * j
                bp = b2[lo + s - 1:lo + s]
                parts.append(kf[lo:lo + s] * jnp.exp2(bp - b2[lo:lo + s]))
                parts.append(qf[lo + s:lo + 2 * s] * jnp.exp2(b2[lo + s:lo + 2 * s] - bp))
            m = jnp.concatenate(parts, axis=0)
        m = m.astype(BF16)
        a_bf = a_bf + mk_ref[k] * _dot_nt(m, m).astype(BF16)

    b_end = b2[tr - 1:tr]
    q_in = (qf * jnp.exp2(b2)).astype(BF16)
    k_out = (kf * jnp.exp2(b_end - b2)).astype(BF16)
    st = st_ref[hd]
    o = _dot(a_bf, v) + _dot_nt(q_in, st.astype(BF16))
    st_ref[hd] = jnp.exp2(b_end) * st + _dot_tn(v, k_out)

    ms = jnp.mean(o * o, axis=-1, keepdims=True)
    on = (o * lax.rsqrt(ms + EPS)).astype(BF16)
    return on * (_silu(gg) * nw.astype(BF16))


def _gla_kernel(q_ref, k_ref, v_ref, gg_ref, a_ref, wa_ref, ba_ref, nw_ref, tc_ref, mk_ref,
                o_ref, st_ref):
    @pl.when(pl.program_id(0) == 0)
    def _():
        st_ref[...] = jnp.zeros_like(st_ref)

    a = a_ref[...]
    a_hi = a.astype(BF16).astype(F32)
    lane = lax.broadcasted_iota(jnp.int32, a.shape, 1)
    a3 = jnp.where((lane >= GLA_RANK) & (lane < 2 * GLA_RANK), a - a_hi, a_hi).astype(BF16)
    w = wa_ref[...]
    w_hi = w.astype(BF16).astype(F32)
    wrow = lax.broadcasted_iota(jnp.int32, w.shape, 0)
    w3 = jnp.where(wrow >= 2 * GLA_RANK, w - w_hi, w_hi).astype(BF16)
    xg = _dot(a3, w3) + ba_ref[...]
    g = (jnp.minimum(xg, 0.0) - jnp.log1p(jnp.exp(-jnp.abs(xg)))) / GLA_TAU
    g2 = g * LOG2E
    tcat = tc_ref[...]
    nw = nw_ref[...]
    for hd in range(GLA_HEADS):
        ks = slice(hd * GLA_DK, (hd + 1) * GLA_DK)
        vs = slice(hd * GLA_DV, (hd + 1) * GLA_DV)
        o_ref[:, vs] = _gla_head(q_ref[:, ks], k_ref[:, ks], v_ref[:, vs], gg_ref[:, vs],
                                 g2[:, ks], nw, tcat, mk_ref, st_ref, hd)


def _gla(z, a_lr, w_alpha_pad, b_alpha, gla_norm_w):
    t = z.shape[0]
    tcat, masks = _gla_constants()
    return pl.pallas_call(
        _gla_kernel,
        grid=(t // GLA_TR,),
        in_specs=[
            pl.BlockSpec((GLA_TR, GLA_KEY), lambda i: (i, OFF_Q // GLA_KEY)),
            pl.BlockSpec((GLA_TR, GLA_KEY), lambda i: (i, OFF_K // GLA_KEY)),
            pl.BlockSpec((GLA_TR, D_GLA), lambda i: (i, OFF_V // D_GLA)),
            pl.BlockSpec((GLA_TR, D_GLA), lambda i: (i, OFF_GG // D_GLA)),
            pl.BlockSpec((GLA_TR, RANK_PAD), lambda i: (i, 0)),
            pl.BlockSpec((RANK_PAD, GLA_KEY), lambda i: (0, 0)),
            pl.BlockSpec((1, GLA_KEY), lambda i: (0, 0)),
            pl.BlockSpec((1, GLA_DV), lambda i: (0, 0)),
            pl.BlockSpec(tcat.shape, lambda i: (0, 0)),
            pl.BlockSpec(masks.shape, lambda i: (0, 0, 0)),
        ],
        out_specs=pl.BlockSpec((GLA_TR, D_GLA), lambda i: (i, 0)),
        out_shape=jax.ShapeDtypeStruct((t, D_GLA), BF16),
        scratch_shapes=[pltpu.VMEM((GLA_HEADS, GLA_DV, GLA_DK), F32)],
        compiler_params=_params(1),
        name="gla",
    )(z, z, z, z, a_lr, w_alpha_pad, b_alpha, gla_norm_w, tcat, masks)


OUT_TM = 256
OUT_NC = 512
OUT_VMEM_LIMIT = 60 * 1024 * 1024


def _outproj_kernel(yp_ref, yg_ref, w_ref, x_ref, gate_ref, fw_ref, o_ref):
    yp = yp_ref[...]
    yg = yg_ref[...]
    ssq = jnp.zeros((OUT_TM, 1), F32)
    for c in range(D_MODEL // OUT_NC):
        cs = slice(c * OUT_NC, (c + 1) * OUT_NC)
        y = _dot(yp, w_ref[:D_POOL, cs]) + _dot(yg, w_ref[D_POOL:, cs])
        xn = x_ref[:, cs] + gate_ref[:, cs] * y
        o_ref[:, cs] = xn
        ssq = ssq + jnp.sum(xn * xn, axis=-1, keepdims=True)
    inv = lax.rsqrt(ssq * (1.0 / D_MODEL) + EPS)
    o_ref[...] = o_ref[...] * inv * fw_ref[...]


def _outproj(y_pool, y_gla, w_out_bf, x2, mod, final_norm_w):
    t = x2.shape[0]
    return pl.pallas_call(
        _outproj_kernel,
        grid=(t // OUT_TM,),
        in_specs=[
            pl.BlockSpec((OUT_TM, D_POOL), lambda i: (i, 0)),
            pl.BlockSpec((OUT_TM, D_GLA), lambda i: (i, 0)),
            pl.BlockSpec((D_MODEL, D_MODEL), lambda i: (0, 0), pipeline_mode=pl.Buffered(1)),
            pl.BlockSpec((OUT_TM, D_MODEL), lambda i: (i, 0)),
            pl.BlockSpec((1, D_MODEL), lambda i: (0, 2)),
            pl.BlockSpec((1, D_MODEL), lambda i: (0, 0)),
        ],
        out_specs=pl.BlockSpec((OUT_TM, D_MODEL), lambda i: (i, 0)),
        out_shape=jax.ShapeDtypeStruct((t, D_MODEL), F32),
        compiler_params=_params(1, OUT_VMEM_LIMIT),
        name="outproj",
    )(y_pool, y_gla, w_out_bf, x2, mod, final_norm_w)


def kernel(x, c, w_ada, b_ada, norm_w, w_in, w_pool, pool_scale, w_alpha, b_alpha,
           gla_norm_w, w_out, final_norm_w):
    bsz, t, d = x.shape
    assert bsz == 1 and d == D_MODEL and w_ada.shape[0] == 1
    x2 = x.reshape(t, d)

    mod = _adaln(c.reshape(d, 1), w_ada[0], b_ada)

    w_in_t = w_in[0].T
    w_main = _castw(w_in_t)
    w_a = jnp.pad(jnp.tile(w_in_t[D_MAIN:], (3, 1)), ((0, RANK_PAD - 3 * GLA_RANK), (0, 0))).astype(BF16)
    h, a_lr = _norm(x2, norm_w, mod, w_a)
    z, w_out_bf = _inproj(h, w_main, w_out[0])

    y_pool = _pool(z, w_pool[0].astype(BF16), pool_scale)

    w_alpha_pad = jnp.pad(jnp.tile(w_alpha[0], (3, 1)), ((0, RANK_PAD - 3 * GLA_RANK), (0, 0)))
    y_gla = _gla(z, a_lr, w_alpha_pad, b_alpha, gla_norm_w)

    out = _outproj(y_pool, y_gla, w_out_bf, x2, mod, final_norm_w.reshape(1, d))
    return out.reshape(bsz, t, d).astype(x.dtype)
```

```python
import math

import jax
import jax.numpy as jnp
import numpy as np
from jax import lax
from jax.experimental import pallas as pl
from jax.experimental.pallas import tpu as pltpu

F32 = jnp.float32
BF16 = jnp.bfloat16

D_MODEL = 4096
D_POOL = 2048
POOL_WINDOWS = (2, 4, 8, 16)
POOL_GROUP = 512
POOL_HALO = 16
D_GLA = 2048
GLA_HEADS = 4
GLA_DV = 512
GLA_DK = 256
GLA_KEY = 1024
GLA_RANK = 16
GLA_TAU = 16.0
EPS = 1e-6
D_MAIN = 2 * D_POOL + 2 * GLA_KEY + 2 * D_GLA
RANK_PAD = 128
LOG2E = math.log2(math.e)

OFF_U, OFF_GP, OFF_Q, OFF_K, OFF_V, OFF_GG = 0, 2048, 4096, 5120, 6144, 8192

VMEM_LIMIT = 56 * 1024 * 1024


def _dot(a, b):
    return jnp.dot(a, b, preferred_element_type=F32)


def _dot_nt(a, b):
    return lax.dot_general(a, b, (((1,), (1,)), ((), ())), preferred_element_type=F32)


def _dot_tn(a, b):
    return lax.dot_general(a, b, (((0,), (0,)), ((), ())), preferred_element_type=F32)


def _split_bf16(a):
    hi = a.astype(BF16)
    lo = (a - hi.astype(F32)).astype(BF16)
    return hi, lo


def _silu(v):
    return v * jax.nn.sigmoid(v)


def _params(n_axes, vmem_limit=VMEM_LIMIT):
    return pltpu.CompilerParams(
        dimension_semantics=("arbitrary",) * n_axes, vmem_limit_bytes=vmem_limit)


ADA_TN = 1024
ADA_KC = 256


def _adaln_kernel(c_ref, w_ref, b_ref, o_ref):
    tn = o_ref.shape[1]

    def body(kk, acc):
        r = pl.multiple_of(kk * ADA_KC, ADA_KC)
        cc = c_ref[pl.ds(r, ADA_KC), :]
        p = _silu(cc) * w_ref[pl.ds(r, ADA_KC), :]
        return acc + p.reshape(ADA_KC // 8, 8, tn).sum(axis=0)

    acc = lax.fori_loop(0, D_MODEL // ADA_KC, body, jnp.zeros((8, tn), F32))
    o_ref[...] = acc.sum(axis=0, keepdims=True) + b_ref[...]


def _adaln(c_col, w_ada, b_ada):
    n = w_ada.shape[1]
    return pl.pallas_call(
        _adaln_kernel,
        grid=(n // ADA_TN,),
        in_specs=[
            pl.BlockSpec((D_MODEL, 1), lambda j: (0, 0)),
            pl.BlockSpec((D_MODEL, ADA_TN), lambda j: (0, j)),
            pl.BlockSpec((1, ADA_TN), lambda j: (0, j)),
        ],
        out_specs=pl.BlockSpec((1, ADA_TN), lambda j: (0, j)),
        out_shape=jax.ShapeDtypeStruct((1, n), F32),
        compiler_params=_params(1),
        name="adaln",
    )(c_col, w_ada, b_ada)


CAST_TM = 512


def _castw_kernel(w_ref, o_ref):
    o_ref[...] = w_ref[...].astype(o_ref.dtype)


def _castw(w_t):
    cols = w_t.shape[1]
    return pl.pallas_call(
        _castw_kernel,
        grid=(D_MAIN // CAST_TM,),
        in_specs=[pl.BlockSpec((CAST_TM, cols), lambda j: (j, 0))],
        out_specs=pl.BlockSpec((CAST_TM, cols), lambda j: (j, 0)),
        out_shape=jax.ShapeDtypeStruct((D_MAIN, cols), BF16),
        compiler_params=_params(1),
        name="castw",
    )(w_t)


IN_TM = 1024
IN_TN = 1280
IN_SLAB = IN_TM // (D_MAIN // IN_TN)
IN_WO_ROWS = 64
IN_VMEM_LIMIT = 60 * 1024 * 1024
Z_LEAD = IN_TM


def _inproj_side_jobs(slot, j, x_ref, nw_ref, shift_ref, scale_ref, wa_ref, wo_ref,
                      a_ref, wo_bf_ref, h_scr):
    xs = x_ref[...]
    ms = jnp.mean(xs * xs, axis=-1, keepdims=True)
    mul = nw_ref[...] * (1.0 + scale_ref[...])
    h = (xs * lax.rsqrt(ms + EPS) * mul + shift_ref[...]).astype(BF16)
    h_scr[slot, pl.ds(pl.multiple_of(j * IN_SLAB, IN_SLAB), IN_SLAB), :] = h
    a_ref[...] = _dot_nt(h, wa_ref[...])
    wo_bf_ref[...] = wo_ref[...].astype(wo_bf_ref.dtype)


def _inproj_kernel(x_ref, nw_ref, shift_ref, scale_ref, wa_ref, w_ref, wo_ref,
                   z_ref, a_ref, wo_bf_ref, h_scr):
    i = pl.program_id(0)
    j = pl.program_id(1)
    side = (x_ref, nw_ref, shift_ref, scale_ref, wa_ref, wo_ref, a_ref, wo_bf_ref, h_scr)

    @pl.when(i == 0)
    def _():
        z_ref[...] = jnp.zeros_like(z_ref)
        _inproj_side_jobs(0, j, *side)

    @pl.when(i > 0)
    def _():
        z_ref[...] = _dot_nt(h_scr[(i + 1) & 1], w_ref[...]).astype(z_ref.dtype)
        _inproj_side_jobs(i & 1, j, *side)


def _inproj(x2, norm_w, mod, w_a, w_main, w_out):
    t = x2.shape[0]
    n_i, n_j = t // IN_TM, D_MAIN // IN_TN
    n_wo = w_out.shape[0] // IN_WO_ROWS
    assert IN_SLAB * n_j == IN_TM and (n_i + 1) * n_j >= n_wo

    def x_slab(i, j):
        return (jnp.minimum(i, n_i - 1) * n_j + j, 0)

    def wo_slab(i, j):
        return (jnp.minimum(i * n_j + j, n_wo - 1), 0)

    return pl.pallas_call(
        _inproj_kernel,
        grid=(n_i + 1, n_j),
        in_specs=[
            pl.BlockSpec((IN_SLAB, D_MODEL), x_slab),
            pl.BlockSpec((1, D_MODEL), lambda i, j: (0, 0)),
            pl.BlockSpec((1, D_MODEL), lambda i, j: (0, 0)),
            pl.BlockSpec((1, D_MODEL), lambda i, j: (0, 1)),
            pl.BlockSpec((RANK_PAD, D_MODEL), lambda i, j: (0, 0)),
            pl.BlockSpec((IN_TN, D_MODEL), lambda i, j: (jnp.where(i == 0, 0, j), 0)),
            pl.BlockSpec((IN_WO_ROWS, w_out.shape[1]), wo_slab),
        ],
        out_specs=[
            pl.BlockSpec((IN_TM, IN_TN), lambda i, j: (i, j)),
            pl.BlockSpec((IN_SLAB, RANK_PAD), lambda i, j: (i * n_j + j, 0)),
            pl.BlockSpec((IN_WO_ROWS, w_out.shape[1]), wo_slab),
        ],
        out_shape=[
            jax.ShapeDtypeStruct((Z_LEAD + t, D_MAIN), BF16),
            jax.ShapeDtypeStruct((t + IN_TM, RANK_PAD), F32),
            jax.ShapeDtypeStruct(w_out.shape, BF16),
        ],
        scratch_shapes=[pltpu.VMEM((2, IN_TM, D_MODEL), BF16)],
        compiler_params=_params(2, IN_VMEM_LIMIT),
        name="inproj",
    )(x2, norm_w, mod, mod, w_a, w_main, w_out)


POOL_TM = 1024
POOL_SUB = 256


def _pool_kernel(u_ref, up_ref, g_ref, w_ref, ps_ref, o_ref):
    grp = pl.program_id(0)
    i = pl.program_id(1)
    ts = POOL_SUB
    win = jnp.left_shift(2, grp)

    row = lax.broadcasted_iota(jnp.int32, (ts, ts), 0)
    col = lax.broadcasted_iota(jnp.int32, (ts, ts), 1)
    in_win = jnp.logical_and(col <= row, col > row - win)
    eye = (row == col).astype(F32)
    hr = lax.broadcasted_iota(jnp.int32, (POOL_HALO, POOL_HALO), 0)
    hc = lax.broadcasted_iota(jnp.int32, (POOL_HALO, POOL_HALO), 1)
    in_halo = hr - hc + POOL_HALO < win
    rows = lax.broadcasted_iota(jnp.int32, (ts, 1), 0)

    def mixing(t0, has_history):
        inv = 1.0 / jnp.minimum(t0 + rows + 1, win).astype(F32)
        band = (jnp.where(in_win, inv, 0.0) - eye).astype(BF16)
        hmask = jnp.logical_and(in_halo, has_history)
        return band, jnp.where(hmask, inv[:POOL_HALO], 0.0).astype(BF16)

    first = mixing(i * POOL_TM, i > 0)
    rest = mixing(i * POOL_TM + ts, True)
    pad = jnp.zeros((ts - POOL_HALO, POOL_GROUP), F32)
    w = w_ref[0]
    ps = ps_ref[...]

    for s in range(POOL_TM // ts):
        r0 = s * ts
        band, hband = first if s == 0 else rest
        halo = up_ref[...] if s == 0 else u_ref[r0 - POOL_HALO:r0, :]
        pooled = _dot(band, u_ref[r0:r0 + ts, :]) + jnp.concatenate([_dot(hband, halo), pad], axis=0)
        mixed = _dot(pooled.astype(BF16), w)
        o_ref[r0:r0 + ts, :] = (mixed * ps).astype(BF16) * _silu(g_ref[r0:r0 + ts, :])


def _pool(z, w_pool_bf, pool_scale):
    t = z.shape[0] - Z_LEAD
    hb = POOL_TM // POOL_HALO
    gb = OFF_GP // POOL_GROUP
    lead, hlead = Z_LEAD // POOL_TM, Z_LEAD // POOL_HALO
    return pl.pallas_call(
        _pool_kernel,
        grid=(len(POOL_WINDOWS), t // POOL_TM),
        in_specs=[
            pl.BlockSpec((POOL_TM, POOL_GROUP), lambda g, i: (lead + i, g)),
            pl.BlockSpec((POOL_HALO, POOL_GROUP), lambda g, i: (hlead + i * hb - 1, g)),
            pl.BlockSpec((POOL_TM, POOL_GROUP), lambda g, i: (lead + i, gb + g)),
            pl.BlockSpec((1, POOL_GROUP, POOL_GROUP), lambda g, i: (g, 0, 0)),
            pl.BlockSpec((1, POOL_GROUP), lambda g, i: (0, g)),
        ],
        out_specs=pl.BlockSpec((POOL_TM, POOL_GROUP), lambda g, i: (i, g)),
        out_shape=jax.ShapeDtypeStruct((t, D_POOL), BF16),
        compiler_params=_params(2),
        name="pool",
    )(z, z, z, w_pool_bf, pool_scale)


GLA_TR = 256
GLA_LEVELS = GLA_TR.bit_length() - 1
GLA_MXU_LEVELS = 3


def _gla_constants():
    r = np.arange(GLA_TR)[:, None]
    t = np.arange(GLA_TR)[None, :]
    mats = []
    for k in range(GLA_MXU_LEVELS):
        half = ((r >> (k + 1)) << (k + 1)) + (1 << k)
        second = ((r >> k) & 1) == 1
        mats.append(np.where(second, (t >= half) & (t <= r), (t > r) & (t < half)))
    mats.append(t <= r)
    tcat = np.concatenate(mats, axis=0).astype(np.float32)
    diff = np.maximum(r ^ t, 1)
    level = np.where(t == r, GLA_LEVELS, np.where(t < r, np.floor(np.log2(diff)).astype(np.int64), -1))
    masks = np.stack([(level == k) for k in range(GLA_LEVELS + 1)]).astype(np.float32)
    return jnp.asarray(tcat, BF16), jnp.asarray(masks, BF16)


def _gla_head(q_bf, k_bf, v, gg, g2, nw, tcat, mk_ref, st_ref, hd):
    tr, nl = GLA_TR, GLA_LEVELS
    g_hi, g_lo = _split_bf16(g2)
    n_small = GLA_MXU_LEVELS * tr
    dall = _dot(tcat[:n_small], g_hi)
    b2 = _dot(tcat[n_small:], g_hi) + _dot(tcat[n_small:], g_lo)
    kf = k_bf.astype(F32)
    qf = q_bf.astype(F32) * (GLA_DK ** -0.5)
    rbit = lax.broadcasted_iota(jnp.int32, (tr, 1), 0)

    a_bf = mk_ref[nl] * _dot_nt(qf.astype(BF16), k_bf).astype(BF16)
    for k in range(nl):
        if k < GLA_MXU_LEVELS:
            side = jnp.where(((rbit >> k) & 1) == 1, qf, kf)
            m = side * jnp.exp2(dall[k * tr:(k + 1) * tr])
        else:
            s = 1 << k
            parts = []
            for j in range(tr // (2 * s)):
                lo = 2 * s * j
                bp = b2[lo + s - 1:lo + s]
                parts.append(kf[lo:lo + s] * jnp.exp2(bp - b2[lo:lo + s]))
                parts.append(qf[lo + s:lo + 2 * s] * jnp.exp2(b2[lo + s:lo + 2 * s] - bp))
            m = jnp.concatenate(parts, axis=0)
        m = m.astype(BF16)
        a_bf = a_bf + mk_ref[k] * _dot_nt(m, m).astype(BF16)

    b_end = b2[tr - 1:tr]
    q_in = (qf * jnp.exp2(b2)).astype(BF16)
    k_out = (kf * jnp.exp2(b_end - b2)).astype(BF16)
    st = st_ref[hd]
    o = _dot(a_bf, v) + _dot_nt(q_in, st.astype(BF16))
    st_ref[hd] = jnp.exp2(b_end) * st + _dot_tn(v, k_out)

    ms = jnp.mean(o * o, axis=-1, keepdims=True)
    on = (o * lax.rsqrt(ms + EPS)).astype(BF16)
    return on * (_silu(gg) * nw.astype(BF16))


def _gla_kernel(q_ref, k_ref, v_ref, gg_ref, a_ref, wa_ref, ba_ref, nw_ref, tc_ref, mk_ref,
                o_ref, st_ref):
    @pl.when(pl.program_id(0) == 0)
    def _():
        st_ref[...] = jnp.zeros_like(st_ref)

    a = a_ref[...]
    a_hi = a.astype(BF16).astype(F32)
    lane = lax.broadcasted_iota(jnp.int32, a.shape, 1)
    a3 = jnp.where((lane >= GLA_RANK) & (lane < 2 * GLA_RANK), a - a_hi, a_hi).astype(BF16)
    w = wa_ref[...]
    w_hi = w.astype(BF16).astype(F32)
    wrow = lax.broadcasted_iota(jnp.int32, w.shape, 0)
    w3 = jnp.where(wrow >= 2 * GLA_RANK, w - w_hi, w_hi).astype(BF16)
    xg = _dot(a3, w3) + ba_ref[...]
    g = (jnp.minimum(xg, 0.0) - jnp.log1p(jnp.exp(-jnp.abs(xg)))) / GLA_TAU
    g2 = g * LOG2E
    tcat = tc_ref[...]
    nw = nw_ref[...]
    for hd in range(GLA_HEADS):
        ks = slice(hd * GLA_DK, (hd + 1) * GLA_DK)
        vs = slice(hd * GLA_DV, (hd + 1) * GLA_DV)
        o_ref[:, vs] = _gla_head(q_ref[:, ks], k_ref[:, ks], v_ref[:, vs], gg_ref[:, vs],
                                 g2[:, ks], nw, tcat, mk_ref, st_ref, hd)


def _gla(z, a_lr, w_alpha_pad, b_alpha, gla_norm_w):
    t = z.shape[0] - Z_LEAD
    lead = Z_LEAD // GLA_TR
    tcat, masks = _gla_constants()
    return pl.pallas_call(
        _gla_kernel,
        grid=(t // GLA_TR,),
        in_specs=[
            pl.BlockSpec((GLA_TR, GLA_KEY), lambda i: (lead + i, OFF_Q // GLA_KEY)),
            pl.BlockSpec((GLA_TR, GLA_KEY), lambda i: (lead + i, OFF_K // GLA_KEY)),
            pl.BlockSpec((GLA_TR, D_GLA), lambda i: (lead + i, OFF_V // D_GLA)),
            pl.BlockSpec((GLA_TR, D_GLA), lambda i: (lead + i, OFF_GG // D_GLA)),
            pl.BlockSpec((GLA_TR, RANK_PAD), lambda i: (i, 0)),
            pl.BlockSpec((RANK_PAD, GLA_KEY), lambda i: (0, 0)),
            pl.BlockSpec((1, GLA_KEY), lambda i: (0, 0)),
            pl.BlockSpec((1, GLA_DV), lambda i: (0, 0)),
            pl.BlockSpec(tcat.shape, lambda i: (0, 0)),
            pl.BlockSpec(masks.shape, lambda i: (0, 0, 0)),
        ],
        out_specs=pl.BlockSpec((GLA_TR, D_GLA), lambda i: (i, 0)),
        out_shape=jax.ShapeDtypeStruct((t, D_GLA), BF16),
        scratch_shapes=[pltpu.VMEM((GLA_HEADS, GLA_DV, GLA_DK), F32)],
        compiler_params=_params(1),
        name="gla",
    )(z, z, z, z, a_lr, w_alpha_pad, b_alpha, gla_norm_w, tcat, masks)


OUT_TM = 256
OUT_NC = 512
OUT_VMEM_LIMIT = 60 * 1024 * 1024


def _outproj_kernel(yp_ref, yg_ref, w_ref, x_ref, gate_ref, fw_ref, o_ref):
    yp = yp_ref[...]
    yg = yg_ref[...]
    ssq = jnp.zeros((OUT_TM, 1), F32)
    for c in range(D_MODEL // OUT_NC):
        cs = slice(c * OUT_NC, (c + 1) * OUT_NC)
        y = _dot(yp, w_ref[:D_POOL, cs]) + _dot(yg, w_ref[D_POOL:, cs])
        xn = x_ref[:, cs] + gate_ref[:, cs] * y
        o_ref[:, cs] = xn
        ssq = ssq + jnp.sum(xn * xn, axis=-1, keepdims=True)
    inv = lax.rsqrt(ssq * (1.0 / D_MODEL) + EPS)
    o_ref[...] = o_ref[...] * inv * fw_ref[...]


def _outproj(y_pool, y_gla, w_out_bf, x2, mod, final_norm_w):
    t = x2.shape[0]
    return pl.pallas_call(
        _outproj_kernel,
        grid=(t // OUT_TM,),
        in_specs=[
            pl.BlockSpec((OUT_TM, D_POOL), lambda i: (i, 0)),
            pl.BlockSpec((OUT_TM, D_GLA), lambda i: (i, 0)),
            pl.BlockSpec((D_MODEL, D_MODEL), lambda i: (0, 0), pipeline_mode=pl.Buffered(1)),
            pl.BlockSpec((OUT_TM, D_MODEL), lambda i: (i, 0)),
            pl.BlockSpec((1, D_MODEL), lambda i: (0, 2)),
            pl.BlockSpec((1, D_MODEL), lambda i: (0, 0)),
        ],
        out_specs=pl.BlockSpec((OUT_TM, D_MODEL), lambda i: (i, 0)),
        out_shape=jax.ShapeDtypeStruct((t, D_MODEL), F32),
        compiler_params=_params(1, OUT_VMEM_LIMIT),
        name="outproj",
    )(y_pool, y_gla, w_out_bf, x2, mod, final_norm_w)


def kernel(x, c, w_ada, b_ada, norm_w, w_in, w_pool, pool_scale, w_alpha, b_alpha,
           gla_norm_w, w_out, final_norm_w):
    bsz, t, d = x.shape
    assert bsz == 1 and d == D_MODEL and w_ada.shape[0] == 1
    x2 = x.reshape(t, d)

    mod = _adaln(c.reshape(d, 1), w_ada[0], b_ada)

    w_in_t = w_in[0].T
    w_main = _castw(w_in_t)
    w_a = jnp.pad(jnp.tile(w_in_t[D_MAIN:], (3, 1)), ((0, RANK_PAD - 3 * GLA_RANK), (0, 0))).astype(BF16)
    z, a_lr, w_out_bf = _inproj(x2, norm_w, mod, w_a, w_main, w_out[0])

    y_pool = _pool(z, w_pool[0].astype(BF16), pool_scale)

    w_alpha_pad = jnp.pad(jnp.tile(w_alpha[0], (3, 1)), ((0, RANK_PAD - 3 * GLA_RANK), (0, 0)))
    y_gla = _gla(z, a_lr, w_alpha_pad, b_alpha, gla_norm_w)

    out = _outproj(y_pool, y_gla, w_out_bf, x2, mod, final_norm_w.reshape(1, d))
    return out.reshape(bsz, t, d).astype(x.dtype)
```

```python
import math

import jax
import jax.numpy as jnp
import numpy as np
from jax import lax
from jax.experimental import pallas as pl
from jax.experimental.pallas import tpu as pltpu

F32 = jnp.float32
BF16 = jnp.bfloat16

D_MODEL = 4096
D_POOL = 2048
POOL_WINDOWS = (2, 4, 8, 16)
POOL_GROUP = 512
POOL_HALO = 16
D_GLA = 2048
GLA_HEADS = 4
GLA_DV = 512
GLA_DK = 256
GLA_KEY = 1024
GLA_RANK = 16
GLA_TAU = 16.0
EPS = 1e-6
D_MAIN = 2 * D_POOL + 2 * GLA_KEY + 2 * D_GLA
RANK_PAD = 128
LOG2E = math.log2(math.e)

OFF_U, OFF_GP, OFF_Q, OFF_K, OFF_V, OFF_GG = 0, 2048, 4096, 5120, 6144, 8192

VMEM_LIMIT = 56 * 1024 * 1024


def _dot(a, b):
    return jnp.dot(a, b, preferred_element_type=F32)


def _dot_nt(a, b):
    return lax.dot_general(a, b, (((1,), (1,)), ((), ())), preferred_element_type=F32)


def _dot_tn(a, b):
    return lax.dot_general(a, b, (((0,), (0,)), ((), ())), preferred_element_type=F32)


def _split_bf16(a):
    hi = a.astype(BF16)
    lo = (a - hi.astype(F32)).astype(BF16)
    return hi, lo


def _silu(v):
    return v * jax.nn.sigmoid(v)


def _params(n_axes, vmem_limit=VMEM_LIMIT):
    return pltpu.CompilerParams(
        dimension_semantics=("arbitrary",) * n_axes, vmem_limit_bytes=vmem_limit)


ADA_TN = 1024
ADA_KC = 256


def _adaln_kernel(c_ref, w_ref, b_ref, o_ref, ca_ref):
    tn = o_ref.shape[1]

    @pl.when(pl.program_id(0) == 0)
    def _():
        ca_ref[...] = _silu(c_ref[...])

    def body(kk, acc):
        r = pl.multiple_of(kk * ADA_KC, ADA_KC)
        p = ca_ref[pl.ds(r, ADA_KC), :] * w_ref[pl.ds(r, ADA_KC), :]
        return acc + p.reshape(ADA_KC // 8, 8, tn).sum(axis=0)

    acc = lax.fori_loop(0, D_MODEL // ADA_KC, body, jnp.zeros((8, tn), F32))
    o_ref[...] = acc.sum(axis=0, keepdims=True) + b_ref[...]


def _adaln(c_col, w_ada, b_ada, n):
    return pl.pallas_call(
        _adaln_kernel,
        grid=(n // ADA_TN,),
        in_specs=[
            pl.BlockSpec((D_MODEL, 1), lambda j: (0, 0)),
            pl.BlockSpec((D_MODEL, ADA_TN), lambda j: (0, j)),
            pl.BlockSpec((1, ADA_TN), lambda j: (0, j)),
        ],
        out_specs=[
            pl.BlockSpec((1, ADA_TN), lambda j: (0, j)),
            pl.BlockSpec((D_MODEL, 1), lambda j: (0, 0)),
        ],
        out_shape=[
            jax.ShapeDtypeStruct((1, n), F32),
            jax.ShapeDtypeStruct((D_MODEL, 1), F32),
        ],
        compiler_params=_params(1),
        name="adaln",
    )(c_col, w_ada, b_ada)


CAST_TM = 512


def _castw_kernel(w_ref, o_ref):
    o_ref[...] = w_ref[...].astype(o_ref.dtype)


def _castw(w_t):
    cols = w_t.shape[1]
    return pl.pallas_call(
        _castw_kernel,
        grid=(D_MAIN // CAST_TM,),
        in_specs=[pl.BlockSpec((CAST_TM, cols), lambda j: (j, 0))],
        out_specs=pl.BlockSpec((CAST_TM, cols), lambda j: (j, 0)),
        out_shape=jax.ShapeDtypeStruct((D_MAIN, cols), BF16),
        compiler_params=_params(1),
        name="castw",
    )(w_t)


IN_TM = 1024
IN_TN = 1280
IN_SLAB = IN_TM // (D_MAIN // IN_TN)
IN_WO_ROWS = 64
IN_VMEM_LIMIT = 60 * 1024 * 1024
Z_LEAD = IN_TM


def _inproj_side_jobs(slot, j, x_ref, nw_ref, shift_ref, scale_ref, wa_ref, wo_ref,
                      a_ref, wo_bf_ref, h_scr):
    xs = x_ref[...]
    ms = jnp.mean(xs * xs, axis=-1, keepdims=True)
    mul = nw_ref[...] * (1.0 + scale_ref[...])
    h = (xs * lax.rsqrt(ms + EPS) * mul + shift_ref[...]).astype(BF16)
    h_scr[slot, pl.ds(pl.multiple_of(j * IN_SLAB, IN_SLAB), IN_SLAB), :] = h
    a_ref[...] = _dot_nt(h, wa_ref[...])
    wo_bf_ref[...] = wo_ref[...].astype(wo_bf_ref.dtype)


def _inproj_kernel(x_ref, nw_ref, shift_ref, scale_ref, wa_ref, w_ref, wo_ref,
                   z_ref, a_ref, wo_bf_ref, h_scr):
    i = pl.program_id(0)
    j = pl.program_id(1)
    side = (x_ref, nw_ref, shift_ref, scale_ref, wa_ref, wo_ref, a_ref, wo_bf_ref, h_scr)

    @pl.when(i == 0)
    def _():
        z_ref[...] = jnp.zeros_like(z_ref)
        _inproj_side_jobs(0, j, *side)

    @pl.when(i > 0)
    def _():
        z_ref[...] = _dot_nt(h_scr[(i + 1) & 1], w_ref[...]).astype(z_ref.dtype)
        _inproj_side_jobs(i & 1, j, *side)


def _inproj(x2, norm_w, mod, w_a, w_main, w_out):
    t = x2.shape[0]
    n_i, n_j = t // IN_TM, D_MAIN // IN_TN
    n_wo = w_out.shape[0] // IN_WO_ROWS
    assert IN_SLAB * n_j == IN_TM and (n_i + 1) * n_j >= n_wo

    def x_slab(i, j):
        return (jnp.minimum(i, n_i - 1) * n_j + j, 0)

    def wo_slab(i, j):
        return (jnp.minimum(i * n_j + j, n_wo - 1), 0)

    return pl.pallas_call(
        _inproj_kernel,
        grid=(n_i + 1, n_j),
        in_specs=[
            pl.BlockSpec((IN_SLAB, D_MODEL), x_slab),
            pl.BlockSpec((1, D_MODEL), lambda i, j: (0, 0)),
            pl.BlockSpec((1, D_MODEL), lambda i, j: (0, 0)),
            pl.BlockSpec((1, D_MODEL), lambda i, j: (0, 1)),
            pl.BlockSpec((RANK_PAD, D_MODEL), lambda i, j: (0, 0)),
            pl.BlockSpec((IN_TN, D_MODEL), lambda i, j: (jnp.where(i == 0, 0, j), 0)),
            pl.BlockSpec((IN_WO_ROWS, w_out.shape[1]), wo_slab),
        ],
        out_specs=[
            pl.BlockSpec((IN_TM, IN_TN), lambda i, j: (i, j)),
            pl.BlockSpec((IN_SLAB, RANK_PAD), lambda i, j: (i * n_j + j, 0)),
            pl.BlockSpec((IN_WO_ROWS, w_out.shape[1]), wo_slab),
        ],
        out_shape=[
            jax.ShapeDtypeStruct((Z_LEAD + t, D_MAIN), BF16),
            jax.ShapeDtypeStruct((t + IN_TM, RANK_PAD), F32),
            jax.ShapeDtypeStruct(w_out.shape, BF16),
        ],
        scratch_shapes=[pltpu.VMEM((2, IN_TM, D_MODEL), BF16)],
        compiler_params=_params(2, IN_VMEM_LIMIT),
        name="inproj",
    )(x2, norm_w, mod, mod, w_a, w_main, w_out)


MIX_TR = 256


def _pool_constants():
    r = np.arange(MIX_TR)[:, None]
    c = np.arange(MIX_TR)[None, :]
    hr = np.arange(POOL_HALO)[:, None]
    hc = np.arange(POOL_HALO)[None, :]
    bands, hbands = [], []
    for w in POOL_WINDOWS:
        bands.append(np.where((c <= r) & (c > r - w), 1.0 / w, 0.0) - (r == c))
        hbands.append(np.where(hr + POOL_HALO - hc < w, 1.0 / w, 0.0))
    return (jnp.asarray(np.stack(bands), BF16), jnp.asarray(np.stack(hbands), BF16))


def _pool_tile(i, u_ref, up_ref, gp_ref, wp_ref, ps_ref, band_ref, hband_ref, y_ref):
    pad = jnp.zeros((MIX_TR - POOL_HALO, POOL_GROUP), F32)
    rows = lax.broadcasted_iota(jnp.int32, (POOL_HALO, 1), 0)
    for g, w in enumerate(POOL_WINDOWS):
        cs = slice(g * POOL_GROUP, (g + 1) * POOL_GROUP)
        u = u_ref[:, cs]
        pooled = _dot(band_ref[g], u) + jnp.concatenate([_dot(hband_ref[g], up_ref[:, cs]), pad], axis=0)
        ratio = jnp.where(i == 0, float(w) / jnp.minimum(rows + 1, w).astype(F32), 1.0)
        u_head = u[:POOL_HALO].astype(F32)
        head = ratio * (pooled[:POOL_HALO] + u_head) - u_head
        pooled = jnp.concatenate([head, pooled[POOL_HALO:]], axis=0)
        mixed = _dot(pooled.astype(BF16), wp_ref[g])
        y_ref[:, cs] = (mixed * ps_ref[:, cs]).astype(BF16) * _silu(gp_ref[:, cs])


GLA_TR = 256
GLA_LEVELS = GLA_TR.bit_length() - 1
GLA_MXU_LEVELS = 3


def _gla_constants():
    r = np.arange(GLA_TR)[:, None]
    t = np.arange(GLA_TR)[None, :]
    mats = []
    for k in range(GLA_MXU_LEVELS):
        half = ((r >> (k + 1)) << (k + 1)) + (1 << k)
        second = ((r >> k) & 1) == 1
        mats.append(np.where(second, (t >= half) & (t <= r), (t > r) & (t < half)))
    mats.append(t <= r)
    tcat = np.concatenate(mats, axis=0).astype(np.float32)
    diff = np.maximum(r ^ t, 1)
    level = np.where(t == r, GLA_LEVELS, np.where(t < r, np.floor(np.log2(diff)).astype(np.int64), -1))
    masks = np.stack([(level == k) for k in range(GLA_LEVELS + 1)]).astype(np.float32)
    return jnp.asarray(tcat, BF16), jnp.asarray(masks, BF16)


def _gla_head(q_bf, k_bf, v, gg, g2, nw, tcat, mk_ref, st_ref, hd):
    tr, nl = GLA_TR, GLA_LEVELS
    g_hi, g_lo = _split_bf16(g2)
    n_small = GLA_MXU_LEVELS * tr
    dall = _dot(tcat[:n_small], g_hi)
    b2 = _dot(tcat[n_small:], g_hi) + _dot(tcat[n_small:], g_lo)
    kf = k_bf.astype(F32)
    qf = q_bf.astype(F32) * (GLA_DK ** -0.5)
    rbit = lax.broadcasted_iota(jnp.int32, (tr, 1), 0)

    a_bf = mk_ref[nl] * _dot_nt(qf.astype(BF16), k_bf).astype(BF16)
    for k in range(nl):
        if k < GLA_MXU_LEVELS:
            side = jnp.where(((rbit >> k) & 1) == 1, qf, kf)
            m = side * jnp.exp2(dall[k * tr:(k + 1) * tr])
        else:
            s = 1 << k
            parts = []
            for j in range(tr // (2 * s)):
                lo = 2 * s * j
                bp = b2[lo + s - 1:lo + s]
                parts.append(kf[lo:lo + s] * jnp.exp2(bp - b2[lo:lo + s]))
                parts.append(qf[lo + s:lo + 2 * s] * jnp.exp2(b2[lo + s:lo + 2 * s] - bp))
            m = jnp.concatenate(parts, axis=0)
        m = m.astype(BF16)
        a_bf = a_bf + mk_ref[k] * _dot_nt(m, m).astype(BF16)

    b_end = b2[tr - 1:tr]
    q_in = (qf * jnp.exp2(b2)).astype(BF16)
    k_out = (kf * jnp.exp2(b_end - b2)).astype(BF16)
    st = st_ref[hd]
    o = _dot(a_bf, v) + _dot_nt(q_in, st.astype(BF16))
    st_ref[hd] = jnp.exp2(b_end) * st + _dot_tn(v, k_out)

    ms = jnp.mean(o * o, axis=-1, keepdims=True)
    on = (o * lax.rsqrt(ms + EPS)).astype(BF16)
    return on * (_silu(gg) * nw.astype(BF16))


MIX_GATE_TN = 128
MIX_GATE_KC = 256


def _gate_block(c_ref, wg_ref, bg_ref, gate_ref):
    acc = jnp.zeros((8, MIX_GATE_TN), F32)
    for kk in range(D_MODEL // MIX_GATE_KC):
        rs = slice(kk * MIX_GATE_KC, (kk + 1) * MIX_GATE_KC)
        p = c_ref[rs, :] * wg_ref[rs, :]
        acc = acc + p.reshape(MIX_GATE_KC // 8, 8, MIX_GATE_TN).sum(axis=0)
    gate_ref[...] = acc.sum(axis=0, keepdims=True) + bg_ref[...]


def _mixer_kernel(u_ref, up_ref, gp_ref, q_ref, k_ref, v_ref, gg_ref, a_ref,
                  wp_ref, ps_ref, band_ref, hband_ref, wa_ref, ba_ref, nw_ref, tc_ref, mk_ref,
                  c_ref, wg_ref, bg_ref, y_ref, gate_ref, st_ref):
    i = pl.program_id(0)

    @pl.when(i == 0)
    def _():
        st_ref[...] = jnp.zeros_like(st_ref)

    _gate_block(c_ref, wg_ref, bg_ref, gate_ref)
    _pool_tile(i, u_ref, up_ref, gp_ref, wp_ref, ps_ref, band_ref, hband_ref, y_ref)

    a = a_ref[...]
    a_hi = a.astype(BF16).astype(F32)
    lane = lax.broadcasted_iota(jnp.int32, a.shape, 1)
    a3 = jnp.where((lane >= GLA_RANK) & (lane < 2 * GLA_RANK), a - a_hi, a_hi).astype(BF16)
    w = wa_ref[...]
    w_hi = w.astype(BF16).astype(F32)
    wrow = lax.broadcasted_iota(jnp.int32, w.shape, 0)
    w3 = jnp.where(wrow >= 2 * GLA_RANK, w - w_hi, w_hi).astype(BF16)
    xg = _dot(a3, w3) + ba_ref[...]
    g = (jnp.minimum(xg, 0.0) - jnp.log1p(jnp.exp(-jnp.abs(xg)))) / GLA_TAU
    g2 = g * LOG2E
    tcat = tc_ref[...]
    nw = nw_ref[...]
    for hd in range(GLA_HEADS):
        ks = slice(hd * GLA_DK, (hd + 1) * GLA_DK)
        vs = slice(hd * GLA_DV, (hd + 1) * GLA_DV)
        ys = slice(D_POOL + hd * GLA_DV, D_POOL + (hd + 1) * GLA_DV)
        y_ref[:, ys] = _gla_head(q_ref[:, ks], k_ref[:, ks], v_ref[:, vs], gg_ref[:, vs],
                                 g2[:, ks], nw, tcat, mk_ref, st_ref, hd)


def _mixer(z, a_lr, w_pool_bf, pool_scale, w_alpha_pad, b_alpha, gla_norm_w, c_act, w_ada, b_ada):
    t = z.shape[0] - Z_LEAD
    n = t // MIX_TR
    assert MIX_TR == GLA_TR and n * MIX_GATE_TN == D_MODEL
    lead = Z_LEAD // MIX_TR
    hpt = MIX_TR // POOL_HALO
    gate0 = 2 * D_MODEL // MIX_GATE_TN
    tcat, masks = _gla_constants()
    bands, hbands = _pool_constants()
    const2 = lambda i: (0, 0)
    const3 = lambda i: (0, 0, 0)
    return pl.pallas_call(
        _mixer_kernel,
        grid=(n,),
        in_specs=[
            pl.BlockSpec((MIX_TR, D_POOL), lambda i: (lead + i, OFF_U // D_POOL)),
            pl.BlockSpec((POOL_HALO, D_POOL), lambda i: ((lead + i) * hpt - 1, OFF_U // D_POOL)),
            pl.BlockSpec((MIX_TR, D_POOL), lambda i: (lead + i, OFF_GP // D_POOL)),
            pl.BlockSpec((MIX_TR, GLA_KEY), lambda i: (lead + i, OFF_Q // GLA_KEY)),
            pl.BlockSpec((MIX_TR, GLA_KEY), lambda i: (lead + i, OFF_K // GLA_KEY)),
            pl.BlockSpec((MIX_TR, D_GLA), lambda i: (lead + i, OFF_V // D_GLA)),
            pl.BlockSpec((MIX_TR, D_GLA), lambda i: (lead + i, OFF_GG // D_GLA)),
            pl.BlockSpec((MIX_TR, RANK_PAD), lambda i: (i, 0)),
            pl.BlockSpec(w_pool_bf.shape, const3),
            pl.BlockSpec((1, D_POOL), const2),
            pl.BlockSpec(bands.shape, const3),
            pl.BlockSpec(hbands.shape, const3),
            pl.BlockSpec((RANK_PAD, GLA_KEY), const2),
            pl.BlockSpec((1, GLA_KEY), const2),
            pl.BlockSpec((1, GLA_DV), const2),
            pl.BlockSpec(tcat.shape, const2),
            pl.BlockSpec(masks.shape, const3),
            pl.BlockSpec((D_MODEL, 1), const2),
            pl.BlockSpec((D_MODEL, MIX_GATE_TN), lambda i: (0, gate0 + i)),
            pl.BlockSpec((1, MIX_GATE_TN), lambda i: (0, gate0 + i)),
        ],
        out_specs=[
            pl.BlockSpec((MIX_TR, D_POOL + D_GLA), lambda i: (i, 0)),
            pl.BlockSpec((1, MIX_GATE_TN), lambda i: (0, i)),
        ],
        out_shape=[
            jax.ShapeDtypeStruct((t, D_POOL + D_GLA), BF16),
            jax.ShapeDtypeStruct((1, D_MODEL), F32),
        ],
        scratch_shapes=[pltpu.VMEM((GLA_HEADS, GLA_DV, GLA_DK), F32)],
        compiler_params=_params(1),
        name="mixer",
    )(z, z, z, z, z, z, z, a_lr, w_pool_bf, pool_scale, bands, hbands, w_alpha_pad, b_alpha,
      gla_norm_w, tcat, masks, c_act, w_ada, b_ada)


OUT_TM = 256
OUT_NC = 512
OUT_VMEM_LIMIT = 60 * 1024 * 1024


def _outproj_kernel(y_ref, w_ref, x_ref, gate_ref, fw_ref, o_ref):
    ym = y_ref[...]
    ssq = jnp.zeros((OUT_TM, 1), F32)
    for c in range(D_MODEL // OUT_NC):
        cs = slice(c * OUT_NC, (c + 1) * OUT_NC)
        y = _dot(ym, w_ref[:, cs])
        xn = x_ref[:, cs] + gate_ref[:, cs] * y
        o_ref[:, cs] = xn
        ssq = ssq + jnp.sum(xn * xn, axis=-1, keepdims=True)
    inv = lax.rsqrt(ssq * (1.0 / D_MODEL) + EPS)
    o_ref[...] = o_ref[...] * inv * fw_ref[...]


def _outproj(y, w_out_bf, x2, gate, final_norm_w):
    t = x2.shape[0]
    return pl.pallas_call(
        _outproj_kernel,
        grid=(t // OUT_TM,),
        in_specs=[
            pl.BlockSpec((OUT_TM, D_POOL + D_GLA), lambda i: (i, 0)),
            pl.BlockSpec((D_MODEL, D_MODEL), lambda i: (0, 0), pipeline_mode=pl.Buffered(1)),
            pl.BlockSpec((OUT_TM, D_MODEL), lambda i: (i, 0)),
            pl.BlockSpec((1, D_MODEL), lambda i: (0, 0)),
            pl.BlockSpec((1, D_MODEL), lambda i: (0, 0)),
        ],
        out_specs=pl.BlockSpec((OUT_TM, D_MODEL), lambda i: (i, 0)),
        out_shape=jax.ShapeDtypeStruct((t, D_MODEL), F32),
        compiler_params=_params(1, OUT_VMEM_LIMIT),
        name="outproj",
    )(y, w_out_bf, x2, gate, final_norm_w)


def kernel(x, c, w_ada, b_ada, norm_w, w_in, w_pool, pool_scale, w_alpha, b_alpha,
           gla_norm_w, w_out, final_norm_w):
    bsz, t, d = x.shape
    assert bsz == 1 and d == D_MODEL and w_ada.shape[0] == 1
    x2 = x.reshape(t, d)

    mod, c_act = _adaln(c.reshape(d, 1), w_ada[0], b_ada, 2 * d)

    w_in_t = w_in[0].T
    w_main = _castw(w_in_t)
    w_a = jnp.pad(jnp.tile(w_in_t[D_MAIN:], (3, 1)), ((0, RANK_PAD - 3 * GLA_RANK), (0, 0))).astype(BF16)
    z, a_lr, w_out_bf = _inproj(x2, norm_w, mod, w_a, w_main, w_out[0])

    w_alpha_pad = jnp.pad(jnp.tile(w_alpha[0], (3, 1)), ((0, RANK_PAD - 3 * GLA_RANK), (0, 0)))
    y, gate = _mixer(z, a_lr, w_pool[0].astype(BF16), pool_scale, w_alpha_pad, b_alpha, gla_norm_w,
                     c_act, w_ada[0], b_ada)

    out = _outproj(y, w_out_bf, x2, gate, final_norm_w.reshape(1, d))
    return out.reshape(bsz, t, d).astype(x.dtype)
```

```python
import math

import jax
import jax.numpy as jnp
import numpy as np
from jax import lax
from jax.experimental import pallas as pl
from jax.experimental.pallas import tpu as pltpu

F32 = jnp.float32
BF16 = jnp.bfloat16

D_MODEL = 4096
D_POOL = 2048
POOL_WINDOWS = (2, 4, 8, 16)
POOL_GROUP = 512
POOL_HALO = 16
D_GLA = 2048
GLA_HEADS = 4
GLA_DV = 512
GLA_DK = 256
GLA_KEY = 1024
GLA_RANK = 16
GLA_TAU = 16.0
EPS = 1e-6
D_MAIN = 2 * D_POOL + 2 * GLA_KEY + 2 * D_GLA
RANK_PAD = 128
LOG2E = math.log2(math.e)

OFF_U, OFF_GP, OFF_Q, OFF_K, OFF_V, OFF_GG = 0, 2048, 4096, 5120, 6144, 8192

VMEM_LIMIT = 56 * 1024 * 1024


def _dot(a, b):
    return jnp.dot(a, b, preferred_element_type=F32)


def _dot_nt(a, b):
    return lax.dot_general(a, b, (((1,), (1,)), ((), ())), preferred_element_type=F32)


def _dot_tn(a, b):
    return lax.dot_general(a, b, (((0,), (0,)), ((), ())), preferred_element_type=F32)


def _split_bf16(a):
    hi = a.astype(BF16)
    lo = (a - hi.astype(F32)).astype(BF16)
    return hi, lo


def _silu(v):
    return v * jax.nn.sigmoid(v)


def _params(n_axes, vmem_limit=VMEM_LIMIT):
    return pltpu.CompilerParams(
        dimension_semantics=("arbitrary",) * n_axes, vmem_limit_bytes=vmem_limit)


ADA_TN = 1024
ADA_KC = 256


def _adaln_kernel(c_ref, w_ref, b_ref, o_ref, ca_ref):
    tn = o_ref.shape[1]

    @pl.when(pl.program_id(0) == 0)
    def _():
        ca_ref[...] = _silu(c_ref[...])

    def body(kk, acc):
        r = pl.multiple_of(kk * ADA_KC, ADA_KC)
        p = ca_ref[pl.ds(r, ADA_KC), :] * w_ref[pl.ds(r, ADA_KC), :]
        return acc + p.reshape(ADA_KC // 8, 8, tn).sum(axis=0)

    acc = lax.fori_loop(0, D_MODEL // ADA_KC, body, jnp.zeros((8, tn), F32))
    o_ref[...] = acc.sum(axis=0, keepdims=True) + b_ref[...]


def _adaln(c_col, w_ada, b_ada, n):
    return pl.pallas_call(
        _adaln_kernel,
        grid=(n // ADA_TN,),
        in_specs=[
            pl.BlockSpec((D_MODEL, 1), lambda j: (0, 0)),
            pl.BlockSpec((D_MODEL, ADA_TN), lambda j: (0, j)),
            pl.BlockSpec((1, ADA_TN), lambda j: (0, j)),
        ],
        out_specs=[
            pl.BlockSpec((1, ADA_TN), lambda j: (0, j)),
            pl.BlockSpec((D_MODEL, 1), lambda j: (0, 0)),
        ],
        out_shape=[
            jax.ShapeDtypeStruct((1, n), F32),
            jax.ShapeDtypeStruct((D_MODEL, 1), F32),
        ],
        compiler_params=_params(1),
        name="adaln",
    )(c_col, w_ada, b_ada)


CAST_TM = 512


def _castw_kernel(w_ref, o_ref):
    o_ref[...] = w_ref[...].astype(o_ref.dtype)


def _castw(w_t):
    cols = w_t.shape[1]
    return pl.pallas_call(
        _castw_kernel,
        grid=(D_MAIN // CAST_TM,),
        in_specs=[pl.BlockSpec((CAST_TM, cols), lambda j: (j, 0))],
        out_specs=pl.BlockSpec((CAST_TM, cols), lambda j: (j, 0)),
        out_shape=jax.ShapeDtypeStruct((D_MAIN, cols), BF16),
        compiler_params=_params(1),
        name="castw",
    )(w_t)


IN_TM = 1024
IN_TN = 1280
IN_SLAB = IN_TM // (D_MAIN // IN_TN)
IN_NSPLIT = 768
IN_WO_ROWS = 64
IN_VMEM_LIMIT = 60 * 1024 * 1024
Z_LEAD = IN_TM


def _inproj_step(h_src, h_dst, j, x_ref, nw_ref, shift_ref, scale_ref, wa_ref, w_ref, wo_ref,
                 z_ref, a_ref, wo_bf_ref):
    xs = x_ref[...]
    ms = jnp.mean(xs * xs, axis=-1, keepdims=True)
    mul = nw_ref[...] * (1.0 + scale_ref[...])
    h = (xs * lax.rsqrt(ms + EPS) * mul + shift_ref[...]).astype(BF16)
    h_dst[pl.ds(pl.multiple_of(j * IN_SLAB, IN_SLAB), IN_SLAB), :] = h
    if h_src is None:
        z_ref[...] = jnp.zeros_like(z_ref)
        a_ref[...] = _dot_nt(h, wa_ref[...])
    else:
        hs = h_src[...]
        z_ref[:, :IN_NSPLIT] = _dot_nt(hs, w_ref[:IN_NSPLIT, :]).astype(z_ref.dtype)
        a_ref[...] = _dot_nt(h, wa_ref[...])
        z_ref[:, IN_NSPLIT:] = _dot_nt(hs, w_ref[IN_NSPLIT:, :]).astype(z_ref.dtype)
    wo_bf_ref[...] = wo_ref[...].astype(wo_bf_ref.dtype)


def _inproj_kernel(x_ref, nw_ref, shift_ref, scale_ref, wa_ref, w_ref, wo_ref,
                   z_ref, a_ref, wo_bf_ref, h_even, h_odd):
    i = pl.program_id(0)
    j = pl.program_id(1)
    refs = (x_ref, nw_ref, shift_ref, scale_ref, wa_ref, w_ref, wo_ref, z_ref, a_ref, wo_bf_ref)
    pl.when(i == 0)(lambda: _inproj_step(None, h_even, j, *refs))
    pl.when((i & 1) == 1)(lambda: _inproj_step(h_even, h_odd, j, *refs))
    pl.when(jnp.logical_and(i > 0, (i & 1) == 0))(lambda: _inproj_step(h_odd, h_even, j, *refs))


def _inproj(x2, norm_w, mod, w_a, w_main, w_out):
    t = x2.shape[0]
    n_i, n_j = t // IN_TM, D_MAIN // IN_TN
    n_wo = w_out.shape[0] // IN_WO_ROWS
    assert IN_SLAB * n_j == IN_TM and (n_i + 1) * n_j >= n_wo

    def x_slab(i, j):
        return (jnp.minimum(i, n_i - 1) * n_j + j, 0)

    def wo_slab(i, j):
        return (jnp.minimum(i * n_j + j, n_wo - 1), 0)

    return pl.pallas_call(
        _inproj_kernel,
        grid=(n_i + 1, n_j),
        in_specs=[
            pl.BlockSpec((IN_SLAB, D_MODEL), x_slab),
            pl.BlockSpec((1, D_MODEL), lambda i, j: (0, 0)),
            pl.BlockSpec((1, D_MODEL), lambda i, j: (0, 0)),
            pl.BlockSpec((1, D_MODEL), lambda i, j: (0, 1)),
            pl.BlockSpec((RANK_PAD, D_MODEL), lambda i, j: (0, 0)),
            pl.BlockSpec((IN_TN, D_MODEL), lambda i, j: (jnp.where(i == 0, 0, j), 0)),
            pl.BlockSpec((IN_WO_ROWS, w_out.shape[1]), wo_slab),
        ],
        out_specs=[
            pl.BlockSpec((IN_TM, IN_TN), lambda i, j: (i, j)),
            pl.BlockSpec((IN_SLAB, RANK_PAD), lambda i, j: (i * n_j + j, 0)),
            pl.BlockSpec((IN_WO_ROWS, w_out.shape[1]), wo_slab),
        ],
        out_shape=[
            jax.ShapeDtypeStruct((Z_LEAD + t, D_MAIN), BF16),
            jax.ShapeDtypeStruct((t + IN_TM, RANK_PAD), F32),
            jax.ShapeDtypeStruct(w_out.shape, BF16),
        ],
        scratch_shapes=[pltpu.VMEM((IN_TM, D_MODEL), BF16), pltpu.VMEM((IN_TM, D_MODEL), BF16)],
        compiler_params=_params(2, IN_VMEM_LIMIT),
        name="inproj",
    )(x2, norm_w, mod, mod, w_a, w_main, w_out)


MIX_TR = 256


def _pool_constants():
    r = np.arange(MIX_TR)[:, None]
    c = np.arange(MIX_TR)[None, :]
    hr = np.arange(POOL_HALO)[:, None]
    hc = np.arange(POOL_HALO)[None, :]
    bands, hbands = [], []
    for w in POOL_WINDOWS:
        bands.append(np.where((c <= r) & (c > r - w), 1.0 / w, 0.0) - (r == c))
        hbands.append(np.where(hr + POOL_HALO - hc < w, 1.0 / w, 0.0))
    return (jnp.asarray(np.stack(bands), BF16), jnp.asarray(np.stack(hbands), BF16))


def _pool_tile(i, u_ref, up_ref, gp_ref, wp_ref, ps_ref, band_ref, hband_ref, y_ref):
    pad = jnp.zeros((MIX_TR - POOL_HALO, POOL_GROUP), F32)
    rows = lax.broadcasted_iota(jnp.int32, (POOL_HALO, 1), 0)
    for g, w in enumerate(POOL_WINDOWS):
        cs = slice(g * POOL_GROUP, (g + 1) * POOL_GROUP)
        u = u_ref[:, cs]
        pooled = _dot(band_ref[g], u) + jnp.concatenate([_dot(hband_ref[g], up_ref[:, cs]), pad], axis=0)
        ratio = jnp.where(i == 0, float(w) / jnp.minimum(rows + 1, w).astype(F32), 1.0)
        u_head = u[:POOL_HALO].astype(F32)
        head = ratio * (pooled[:POOL_HALO] + u_head) - u_head
        pooled = jnp.concatenate([head, pooled[POOL_HALO:]], axis=0)
        mixed = _dot(pooled.astype(BF16), wp_ref[g])
        y_ref[:, cs] = (mixed * ps_ref[:, cs]).astype(BF16) * _silu(gp_ref[:, cs])


GLA_TR = 256
GLA_LEVELS = GLA_TR.bit_length() - 1
GLA_MXU_LEVELS = 4


def _gla_constants():
    r = np.arange(GLA_TR)[:, None]
    t = np.arange(GLA_TR)[None, :]
    mats = []
    for k in range(GLA_MXU_LEVELS):
        half = ((r >> (k + 1)) << (k + 1)) + (1 << k)
        second = ((r >> k) & 1) == 1
        mats.append(np.where(second, (t >= half) & (t <= r), (t > r) & (t < half)))
    mats.append(t <= r)
    tcat = np.concatenate(mats, axis=0).astype(np.float32)
    diff = np.maximum(r ^ t, 1)
    level = np.where(t == r, GLA_LEVELS, np.where(t < r, np.floor(np.log2(diff)).astype(np.int64), -1))
    masks = np.stack([(level == k) for k in range(GLA_LEVELS + 1)]).astype(np.float32) * GLA_DK ** -0.5
    return jnp.asarray(tcat, BF16), jnp.asarray(masks, BF16)


def _gla_head(q_bf, k_bf, v, gg, g2, nw, tcat, mk_ref, st_ref, hd):
    tr, nl = GLA_TR, GLA_LEVELS
    g_hi, g_lo = _split_bf16(g2)
    n_small = GLA_MXU_LEVELS * tr
    dall = _dot(tcat[:n_small], g_hi)
    b2 = _dot(tcat[n_small:], g_hi) + _dot(tcat[n_small:], g_lo)
    rbit = lax.broadcasted_iota(jnp.int32, (tr, 1), 0)

    a_bf = mk_ref[nl] * _dot_nt(q_bf, k_bf).astype(BF16)
    for k in range(nl):
        if k < GLA_MXU_LEVELS:
            side = jnp.where(((rbit >> k) & 1) == 1, q_bf, k_bf)
            m = side * jnp.exp2(dall[k * tr:(k + 1) * tr]).astype(BF16)
        else:
            s = 1 << k
            parts = []
            for j in range(tr // (2 * s)):
                lo = 2 * s * j
                bp = b2[lo + s - 1:lo + s]
                parts.append(k_bf[lo:lo + s] * jnp.exp2(bp - b2[lo:lo + s]).astype(BF16))
                parts.append(q_bf[lo + s:lo + 2 * s] * jnp.exp2(b2[lo + s:lo + 2 * s] - bp).astype(BF16))
            m = jnp.concatenate(parts, axis=0)
        a_bf = a_bf + mk_ref[k] * _dot_nt(m, m).astype(BF16)

    b_end = b2[tr - 1:tr]
    q_in = q_bf * jnp.exp2(b2).astype(BF16)
    k_out = k_bf * jnp.exp2((b_end - 0.5 * math.log2(GLA_DK)) - b2).astype(BF16)
    st = st_ref[hd]
    o = _dot(a_bf, v) + _dot_nt(q_in, st.astype(BF16))
    st_ref[hd] = jnp.exp2(b_end) * st + _dot_tn(v, k_out)

    ms = jnp.mean(o * o, axis=-1, keepdims=True)
    on = (o * lax.rsqrt(ms + EPS)).astype(BF16)
    return on * (_silu(gg) * nw)


MIX_GATE_TN = 128
MIX_GATE_KC = 256


def _gate_block(c_ref, wg_ref, bg_ref, gate_ref):
    acc = jnp.zeros((8, MIX_GATE_TN), F32)
    for kk in range(D_MODEL // MIX_GATE_KC):
        rs = slice(kk * MIX_GATE_KC, (kk + 1) * MIX_GATE_KC)
        p = c_ref[rs, :] * wg_ref[rs, :]
        acc = acc + p.reshape(MIX_GATE_KC // 8, 8, MIX_GATE_TN).sum(axis=0)
    gate_ref[...] = acc.sum(axis=0, keepdims=True) + bg_ref[...]


def _mixer_kernel(u_ref, up_ref, gp_ref, q_ref, k_ref, v_ref, gg_ref, a_ref,
                  wp_ref, ps_ref, band_ref, hband_ref, wa_ref, ba_ref, nw_ref, tc_ref, mk_ref,
                  c_ref, wg_ref, bg_ref, y_ref, gate_ref, st_ref):
    i = pl.program_id(0)

    @pl.when(i == 0)
    def _():
        st_ref[...] = jnp.zeros_like(st_ref)

    _gate_block(c_ref, wg_ref, bg_ref, gate_ref)
    _pool_tile(i, u_ref, up_ref, gp_ref, wp_ref, ps_ref, band_ref, hband_ref, y_ref)

    a = a_ref[...]
    a_hi = a.astype(BF16).astype(F32)
    lane = lax.broadcasted_iota(jnp.int32, a.shape, 1)
    a3 = jnp.where((lane >= GLA_RANK) & (lane < 2 * GLA_RANK), a - a_hi, a_hi).astype(BF16)
    w = wa_ref[...]
    w_hi = w.astype(BF16).astype(F32)
    wrow = lax.broadcasted_iota(jnp.int32, w.shape, 0)
    w3 = jnp.where(wrow >= 2 * GLA_RANK, w - w_hi, w_hi).astype(BF16)
    xl = (_dot(a3, w3) + ba_ref[...]) * LOG2E
    g2 = (jnp.minimum(xl, 0.0) - jnp.log2(1.0 + jnp.exp2(-jnp.abs(xl)))) * (1.0 / GLA_TAU)
    tcat = tc_ref[...]
    nw = nw_ref[...].astype(BF16)
    for hd in range(GLA_HEADS):
        ks = slice(hd * GLA_DK, (hd + 1) * GLA_DK)
        vs = slice(hd * GLA_DV, (hd + 1) * GLA_DV)
        ys = slice(D_POOL + hd * GLA_DV, D_POOL + (hd + 1) * GLA_DV)
        y_ref[:, ys] = _gla_head(q_ref[:, ks], k_ref[:, ks], v_ref[:, vs], gg_ref[:, vs],
                                 g2[:, ks], nw, tcat, mk_ref, st_ref, hd)


def _mixer(z, a_lr, w_pool_bf, pool_scale, w_alpha_pad, b_alpha, gla_norm_w, c_act, w_ada, b_ada):
    t = z.shape[0] - Z_LEAD
    n = t // MIX_TR
    assert MIX_TR == GLA_TR and n * MIX_GATE_TN == D_MODEL
    lead = Z_LEAD // MIX_TR
    hpt = MIX_TR // POOL_HALO
    gate0 = 2 * D_MODEL // MIX_GATE_TN
    tcat, masks = _gla_constants()
    bands, hbands = _pool_constants()
    const2 = lambda i: (0, 0)
    const3 = lambda i: (0, 0, 0)
    return pl.pallas_call(
        _mixer_kernel,
        grid=(n,),
        in_specs=[
            pl.BlockSpec((MIX_TR, D_POOL), lambda i: (lead + i, OFF_U // D_POOL)),
            pl.BlockSpec((POOL_HALO, D_POOL), lambda i: ((lead + i) * hpt - 1, OFF_U // D_POOL)),
            pl.BlockSpec((MIX_TR, D_POOL), lambda i: (lead + i, OFF_GP // D_POOL)),
            pl.BlockSpec((MIX_TR, GLA_KEY), lambda i: (lead + i, OFF_Q // GLA_KEY)),
            pl.BlockSpec((MIX_TR, GLA_KEY), lambda i: (lead + i, OFF_K // GLA_KEY)),
            pl.BlockSpec((MIX_TR, D_GLA), lambda i: (lead + i, OFF_V // D_GLA)),
            pl.BlockSpec((MIX_TR, D_GLA), lambda i: (lead + i, OFF_GG // D_GLA)),
            pl.BlockSpec((MIX_TR, RANK_PAD), lambda i: (i, 0)),
            pl.BlockSpec(w_pool_bf.shape, const3),
            pl.BlockSpec((1, D_POOL), const2),
            pl.BlockSpec(bands.shape, const3),
            pl.BlockSpec(hbands.shape, const3),
            pl.BlockSpec((RANK_PAD, GLA_KEY), const2),
            pl.BlockSpec((1, GLA_KEY), const2),
            pl.BlockSpec((1, GLA_DV), const2),
            pl.BlockSpec(tcat.shape, const2),
            pl.BlockSpec(masks.shape, const3),
            pl.BlockSpec((D_MODEL, 1), const2),
            pl.BlockSpec((D_MODEL, MIX_GATE_TN), lambda i: (0, gate0 + i)),
            pl.BlockSpec((1, MIX_GATE_TN), lambda i: (0, gate0 + i)),
        ],
        out_specs=[
            pl.BlockSpec((MIX_TR, D_POOL + D_GLA), lambda i: (i, 0)),
            pl.BlockSpec((1, MIX_GATE_TN), lambda i: (0, i)),
        ],
        out_shape=[
            jax.ShapeDtypeStruct((t, D_POOL + D_GLA), BF16),
            jax.ShapeDtypeStruct((1, D_MODEL), F32),
        ],
        scratch_shapes=[pltpu.VMEM((GLA_HEADS, GLA_DV, GLA_DK), F32)],
        compiler_params=_params(1),
        name="mixer",
    )(z, z, z, z, z, z, z, a_lr, w_pool_bf, pool_scale, bands, hbands, w_alpha_pad, b_alpha,
      gla_norm_w, tcat, masks, c_act, w_ada, b_ada)


OUT_TM = 256
OUT_NC = 512
OUT_VMEM_LIMIT = 60 * 1024 * 1024


def _outproj_kernel(y_ref, w_ref, x_ref, gate_ref, fw_ref, o_ref):
    ym = y_ref[...]
    ssq = jnp.zeros((OUT_TM, 1), F32)
    for c in range(D_MODEL // OUT_NC):
        cs = slice(c * OUT_NC, (c + 1) * OUT_NC)
        y = _dot(ym, w_ref[:, cs])
        xn = x_ref[:, cs] + gate_ref[:, cs] * y
        o_ref[:, cs] = xn
        ssq = ssq + jnp.sum(xn * xn, axis=-1, keepdims=True)
    inv = lax.rsqrt(ssq * (1.0 / D_MODEL) + EPS)
    o_ref[...] = o_ref[...] * inv * fw_ref[...]


def _outproj(y, w_out_bf, x2, gate, final_norm_w):
    t = x2.shape[0]
    return pl.pallas_call(
        _outproj_kernel,
        grid=(t // OUT_TM,),
        in_specs=[
            pl.BlockSpec((OUT_TM, D_POOL + D_GLA), lambda i: (i, 0)),
            pl.BlockSpec((D_MODEL, D_MODEL), lambda i: (0, 0), pipeline_mode=pl.Buffered(1)),
            pl.BlockSpec((OUT_TM, D_MODEL), lambda i: (i, 0)),
            pl.BlockSpec((1, D_MODEL), lambda i: (0, 0)),
            pl.BlockSpec((1, D_MODEL), lambda i: (0, 0)),
        ],
        out_specs=pl.BlockSpec((OUT_TM, D_MODEL), lambda i: (i, 0)),
        out_shape=jax.ShapeDtypeStruct((t, D_MODEL), F32),
        compiler_params=_params(1, OUT_VMEM_LIMIT),
        name="outproj",
    )(y, w_out_bf, x2, gate, final_norm_w)


def kernel(x, c, w_ada, b_ada, norm_w, w_in, w_pool, pool_scale, w_alpha, b_alpha,
           gla_norm_w, w_out, final_norm_w):
    bsz, t, d = x.shape
    assert bsz == 1 and d == D_MODEL and w_ada.shape[0] == 1
    x2 = x.reshape(t, d)

    mod, c_act = _adaln(c.reshape(d, 1), w_ada[0], b_ada, 2 * d)

    w_in_t = w_in[0].T
    w_main = _castw(w_in_t)
    w_a = jnp.pad(jnp.tile(w_in_t[D_MAIN:], (3, 1)), ((0, RANK_PAD - 3 * GLA_RANK), (0, 0))).astype(BF16)
    z, a_lr, w_out_bf = _inproj(x2, norm_w, mod, w_a, w_main, w_out[0])

    w_alpha_pad = jnp.pad(jnp.tile(w_alpha[0], (3, 1)), ((0, RANK_PAD - 3 * GLA_RANK), (0, 0)))
    y, gate = _mixer(z, a_lr, w_pool[0].astype(BF16), pool_scale, w_alpha_pad, b_alpha, gla_norm_w,
                     c_act, w_ada[0], b_ada)

    out = _outproj(y, w_out_bf, x2, gate, final_norm_w.reshape(1, d))
    return out.reshape(bsz, t, d).astype(x.dtype)
```

```python
import math

import jax
import jax.numpy as jnp
import numpy as np
from jax import lax
from jax.experimental import pallas as pl
from jax.experimental.pallas import tpu as pltpu

F32 = jnp.float32
BF16 = jnp.bfloat16

D_MODEL = 4096
D_POOL = 2048
POOL_WINDOWS = (2, 4, 8, 16)
POOL_GROUP = 512
POOL_HALO = 16
D_GLA = 2048
GLA_HEADS = 4
GLA_DV = 512
GLA_DK = 256
GLA_KEY = 1024
GLA_RANK = 16
GLA_TAU = 16.0
EPS = 1e-6
D_MAIN = 2 * D_POOL + 2 * GLA_KEY + 2 * D_GLA
RANK_PAD = 128
LOG2E = math.log2(math.e)

OFF_U, OFF_GP, OFF_Q, OFF_K, OFF_V, OFF_GG = 0, 2048, 4096, 5120, 6144, 8192

VMEM_LIMIT = 56 * 1024 * 1024


def _dot(a, b):
    return jnp.dot(a, b, preferred_element_type=F32)


def _dot_nt(a, b):
    return lax.dot_general(a, b, (((1,), (1,)), ((), ())), preferred_element_type=F32)


def _dot_tn(a, b):
    return lax.dot_general(a, b, (((0,), (0,)), ((), ())), preferred_element_type=F32)


def _split_bf16(a):
    hi = a.astype(BF16)
    lo = (a - hi.astype(F32)).astype(BF16)
    return hi, lo


def _silu(v):
    return v * jax.nn.sigmoid(v)


def _params(n_axes, vmem_limit=VMEM_LIMIT):
    return pltpu.CompilerParams(
        dimension_semantics=("arbitrary",) * n_axes, vmem_limit_bytes=vmem_limit)


ADA_TN = 512
ADA_KC = 256
CAST_TM = 512


def _prelude_kernel(c_ref, w_ref, b_ref, wt_ref, o_ref, ca_ref, wbf_ref):
    tn = o_ref.shape[1]

    @pl.when(pl.program_id(0) == 0)
    def _():
        ca_ref[...] = _silu(c_ref[...])

    wbf_ref[...] = wt_ref[...].astype(wbf_ref.dtype)

    def body(kk, acc):
        r = pl.multiple_of(kk * ADA_KC, ADA_KC)
        p = ca_ref[pl.ds(r, ADA_KC), :] * w_ref[pl.ds(r, ADA_KC), :]
        return acc + p.reshape(ADA_KC // 8, 8, tn).sum(axis=0)

    acc = lax.fori_loop(0, D_MODEL // ADA_KC, body, jnp.zeros((8, tn), F32))
    o_ref[...] = acc.sum(axis=0, keepdims=True) + b_ref[...]


def _prelude(c_col, w_ada, b_ada, n, w_t):
    n_ada, n_cast = n // ADA_TN, D_MAIN // CAST_TM
    assert n_cast >= n_ada
    ada_blk = lambda j: (0, jnp.minimum(j, n_ada - 1))
    return pl.pallas_call(
        _prelude_kernel,
        grid=(n_cast,),
        in_specs=[
            pl.BlockSpec((D_MODEL, 1), lambda j: (0, 0)),
            pl.BlockSpec((D_MODEL, ADA_TN), ada_blk),
            pl.BlockSpec((1, ADA_TN), ada_blk),
            pl.BlockSpec((CAST_TM, w_t.shape[1]), lambda j: (j, 0)),
        ],
        out_specs=[
            pl.BlockSpec((1, ADA_TN), ada_blk),
            pl.BlockSpec((D_MODEL, 1), lambda j: (0, 0)),
            pl.BlockSpec((CAST_TM, w_t.shape[1]), lambda j: (j, 0)),
        ],
        out_shape=[
            jax.ShapeDtypeStruct((1, n), F32),
            jax.ShapeDtypeStruct((D_MODEL, 1), F32),
            jax.ShapeDtypeStruct((D_MAIN, w_t.shape[1]), BF16),
        ],
        compiler_params=_params(1),
        name="prelude",
    )(c_col, w_ada, b_ada, w_t)


IN_TM = 1024
IN_TN = 1280
IN_SLAB = IN_TM // (D_MAIN // IN_TN)
IN_NSPLIT = 768
IN_WO_ROWS = 64
IN_VMEM_LIMIT = 60 * 1024 * 1024
Z_LEAD = IN_TM


def _inproj_step(h_src, h_dst, j, x_ref, nw_ref, shift_ref, scale_ref, wa_ref, w_ref, wo_ref,
                 z_ref, a_ref, wo_bf_ref):
    xs = x_ref[...]
    ms = jnp.mean(xs * xs, axis=-1, keepdims=True)
    mul = nw_ref[...] * (1.0 + scale_ref[...])
    h = (xs * lax.rsqrt(ms + EPS) * mul + shift_ref[...]).astype(BF16)
    h_dst[pl.ds(pl.multiple_of(j * IN_SLAB, IN_SLAB), IN_SLAB), :] = h
    if h_src is None:
        z_ref[...] = jnp.zeros_like(z_ref)
        a_ref[...] = _dot_nt(h, wa_ref[...])
    else:
        hs = h_src[...]
        z_ref[:, :IN_NSPLIT] = _dot_nt(hs, w_ref[:IN_NSPLIT, :]).astype(z_ref.dtype)
        a_ref[...] = _dot_nt(h, wa_ref[...])
        z_ref[:, IN_NSPLIT:] = _dot_nt(hs, w_ref[IN_NSPLIT:, :]).astype(z_ref.dtype)
    wo_bf_ref[...] = wo_ref[...].astype(wo_bf_ref.dtype)


def _inproj_kernel(x_ref, nw_ref, shift_ref, scale_ref, wa_ref, w_ref, wo_ref,
                   z_ref, a_ref, wo_bf_ref, h_even, h_odd):
    i = pl.program_id(0)
    j = pl.program_id(1)
    refs = (x_ref, nw_ref, shift_ref, scale_ref, wa_ref, w_ref, wo_ref, z_ref, a_ref, wo_bf_ref)
    pl.when(i == 0)(lambda: _inproj_step(None, h_even, j, *refs))
    pl.when((i & 1) == 1)(lambda: _inproj_step(h_even, h_odd, j, *refs))
    pl.when(jnp.logical_and(i > 0, (i & 1) == 0))(lambda: _inproj_step(h_odd, h_even, j, *refs))


def _inproj(x2, norm_w, mod, w_a, w_main, w_out):
    t = x2.shape[0]
    n_i, n_j = t // IN_TM, D_MAIN // IN_TN
    n_wo = w_out.shape[0] // IN_WO_ROWS
    assert IN_SLAB * n_j == IN_TM and (n_i + 1) * n_j >= n_wo

    def x_slab(i, j):
        return (jnp.minimum(i, n_i - 1) * n_j + j, 0)

    def wo_slab(i, j):
        return (jnp.minimum(i * n_j + j, n_wo - 1), 0)

    return pl.pallas_call(
        _inproj_kernel,
        grid=(n_i + 1, n_j),
        in_specs=[
            pl.BlockSpec((IN_SLAB, D_MODEL), x_slab),
            pl.BlockSpec((1, D_MODEL), lambda i, j: (0, 0)),
            pl.BlockSpec((1, D_MODEL), lambda i, j: (0, 0)),
            pl.BlockSpec((1, D_MODEL), lambda i, j: (0, 1)),
            pl.BlockSpec((RANK_PAD, D_MODEL), lambda i, j: (0, 0)),
            pl.BlockSpec((IN_TN, D_MODEL), lambda i, j: (jnp.where(i == 0, 0, j), 0)),
            pl.BlockSpec((IN_WO_ROWS, w_out.shape[1]), wo_slab),
        ],
        out_specs=[
            pl.BlockSpec((IN_TM, IN_TN), lambda i, j: (i, j)),
            pl.BlockSpec((IN_SLAB, RANK_PAD), lambda i, j: (i * n_j + j, 0)),
            pl.BlockSpec((IN_WO_ROWS, w_out.shape[1]), wo_slab),
        ],
        out_shape=[
            jax.ShapeDtypeStruct((Z_LEAD + t, D_MAIN), BF16),
            jax.ShapeDtypeStruct((t + IN_TM, RANK_PAD), F32),
            jax.ShapeDtypeStruct(w_out.shape, BF16),
        ],
        scratch_shapes=[pltpu.VMEM((IN_TM, D_MODEL), BF16), pltpu.VMEM((IN_TM, D_MODEL), BF16)],
        compiler_params=_params(2, IN_VMEM_LIMIT),
        name="inproj",
    )(x2, norm_w, mod, mod, w_a, w_main, w_out)


MIX_TR = 256


def _pool_constants():
    r = np.arange(MIX_TR)[:, None]
    c = np.arange(MIX_TR)[None, :]
    hr = np.arange(POOL_HALO)[:, None]
    hc = np.arange(POOL_HALO)[None, :]
    bands, hbands = [], []
    for w in POOL_WINDOWS:
        bands.append(np.where((c <= r) & (c > r - w), 1.0 / w, 0.0) - (r == c))
        hbands.append(np.where(hr + POOL_HALO - hc < w, 1.0 / w, 0.0))
    return (jnp.asarray(np.stack(bands), BF16), jnp.asarray(np.stack(hbands), BF16))


def _pool_tile(i, u_ref, up_ref, gp_ref, wp_ref, ps_ref, band_ref, hband_ref, y_ref):
    pad = jnp.zeros((MIX_TR - POOL_HALO, POOL_GROUP), F32)
    rows = lax.broadcasted_iota(jnp.int32, (POOL_HALO, 1), 0)
    for g, w in enumerate(POOL_WINDOWS):
        cs = slice(g * POOL_GROUP, (g + 1) * POOL_GROUP)
        u = u_ref[:, cs]
        pooled = _dot(band_ref[g], u) + jnp.concatenate([_dot(hband_ref[g], up_ref[:, cs]), pad], axis=0)
        ratio = jnp.where(i == 0, float(w) / jnp.minimum(rows + 1, w).astype(F32), 1.0)
        u_head = u[:POOL_HALO].astype(F32)
        head = ratio * (pooled[:POOL_HALO] + u_head) - u_head
        pooled = jnp.concatenate([head, pooled[POOL_HALO:]], axis=0)
        mixed = _dot(pooled.astype(BF16), wp_ref[g])
        y_ref[:, cs] = (mixed * ps_ref[:, cs]).astype(BF16) * _silu(gp_ref[:, cs])


GLA_TR = 256
GLA_LEVELS = GLA_TR.bit_length() - 1
GLA_MXU_LEVELS = 4


def _gla_constants():
    r = np.arange(GLA_TR)[:, None]
    t = np.arange(GLA_TR)[None, :]
    mats = []
    for k in range(GLA_MXU_LEVELS):
        half = ((r >> (k + 1)) << (k + 1)) + (1 << k)
        second = ((r >> k) & 1) == 1
        mats.append(np.where(second, (t >= half) & (t <= r), (t > r) & (t < half)))
    mats.append(t <= r)
    tcat = np.concatenate(mats, axis=0).astype(np.float32)
    diff = np.maximum(r ^ t, 1)
    level = np.where(t == r, GLA_LEVELS, np.where(t < r, np.floor(np.log2(diff)).astype(np.int64), -1))
    masks = np.stack([(level == k) for k in range(GLA_LEVELS + 1)]).astype(np.float32) * GLA_DK ** -0.5
    return jnp.asarray(tcat, BF16), jnp.asarray(masks, BF16)


def _gla_head(q_bf, k_bf, v, gg, g2, nw, tcat, mk_ref, st_ref, hd):
    tr, nl = GLA_TR, GLA_LEVELS
    g_hi, g_lo = _split_bf16(g2)
    n_small = GLA_MXU_LEVELS * tr
    dall = _dot(tcat[:n_small], g_hi)
    b2 = _dot(tcat[n_small:], g_hi) + _dot(tcat[n_small:], g_lo)
    rbit = lax.broadcasted_iota(jnp.int32, (tr, 1), 0)

    a_bf = mk_ref[nl] * _dot_nt(q_bf, k_bf).astype(BF16)
    for k in range(nl):
        if k < GLA_MXU_LEVELS:
            side = jnp.where(((rbit >> k) & 1) == 1, q_bf, k_bf)
            m = side * jnp.exp2(dall[k * tr:(k + 1) * tr]).astype(BF16)
        else:
            s = 1 << k
            parts = []
            for j in range(tr // (2 * s)):
                lo = 2 * s * j
                bp = b2[lo + s - 1:lo + s]
                parts.append(k_bf[lo:lo + s] * jnp.exp2(bp - b2[lo:lo + s]).astype(BF16))
                parts.append(q_bf[lo + s:lo + 2 * s] * jnp.exp2(b2[lo + s:lo + 2 * s] - bp).astype(BF16))
            m = jnp.concatenate(parts, axis=0)
        a_bf = a_bf + mk_ref[k] * _dot_nt(m, m).astype(BF16)

    b_end = b2[tr - 1:tr]
    q_in = q_bf * jnp.exp2(b2).astype(BF16)
    k_out = k_bf * jnp.exp2((b_end - 0.5 * math.log2(GLA_DK)) - b2).astype(BF16)
    st = st_ref[hd]
    o = _dot(a_bf, v) + _dot_nt(q_in, st.astype(BF16))
    st_ref[hd] = jnp.exp2(b_end) * st + _dot_tn(v, k_out)

    ms = jnp.mean(o * o, axis=-1, keepdims=True)
    on = (o * lax.rsqrt(ms + EPS)).astype(BF16)
    return on * (_silu(gg) * nw)


MIX_GATE_TN = 128
MIX_GATE_KC = 256


def _gate_block(c_ref, wg_ref, bg_ref, gate_ref):
    acc = jnp.zeros((8, MIX_GATE_TN), F32)
    for kk in range(D_MODEL // MIX_GATE_KC):
        rs = slice(kk * MIX_GATE_KC, (kk + 1) * MIX_GATE_KC)
        p = c_ref[rs, :] * wg_ref[rs, :]
        acc = acc + p.reshape(MIX_GATE_KC // 8, 8, MIX_GATE_TN).sum(axis=0)
    gate_ref[...] = acc.sum(axis=0, keepdims=True) + bg_ref[...]


def _mixer_kernel(u_ref, up_ref, gp_ref, q_ref, k_ref, v_ref, gg_ref, a_ref,
                  wp_ref, ps_ref, band_ref, hband_ref, wa_ref, ba_ref, nw_ref, tc_ref, mk_ref,
                  c_ref, wg_ref, bg_ref, y_ref, gate_ref, st_ref):
    i = pl.program_id(0)

    @pl.when(i == 0)
    def _():
        st_ref[...] = jnp.zeros_like(st_ref)

    _gate_block(c_ref, wg_ref, bg_ref, gate_ref)
    _pool_tile(i, u_ref, up_ref, gp_ref, wp_ref, ps_ref, band_ref, hband_ref, y_ref)

    a = a_ref[...]
    a_hi = a.astype(BF16).astype(F32)
    lane = lax.broadcasted_iota(jnp.int32, a.shape, 1)
    a3 = jnp.where((lane >= GLA_RANK) & (lane < 2 * GLA_RANK), a - a_hi, a_hi).astype(BF16)
    w = wa_ref[...]
    w_hi = w.astype(BF16).astype(F32)
    wrow = lax.broadcasted_iota(jnp.int32, w.shape, 0)
    w3 = jnp.where(wrow >= 2 * GLA_RANK, w - w_hi, w_hi).astype(BF16)
    xl = (_dot(a3, w3) + ba_ref[...]) * LOG2E
    g2 = (jnp.minimum(xl, 0.0) - jnp.log2(1.0 + jnp.exp2(-jnp.abs(xl)))) * (1.0 / GLA_TAU)
    tcat = tc_ref[...]
    nw = nw_ref[...].astype(BF16)
    for hd in range(GLA_HEADS):
        ks = slice(hd * GLA_DK, (hd + 1) * GLA_DK)
        vs = slice(hd * GLA_DV, (hd + 1) * GLA_DV)
        ys = slice(D_POOL + hd * GLA_DV, D_POOL + (hd + 1) * GLA_DV)
        y_ref[:, ys] = _gla_head(q_ref[:, ks], k_ref[:, ks], v_ref[:, vs], gg_ref[:, vs],
                                 g2[:, ks], nw, tcat, mk_ref, st_ref, hd)


def _mixer(z, a_lr, w_pool_bf, pool_scale, w_alpha_pad, b_alpha, gla_norm_w, c_act, w_ada, b_ada):
    t = z.shape[0] - Z_LEAD
    n = t // MIX_TR
    assert MIX_TR == GLA_TR and n * MIX_GATE_TN == D_MODEL
    lead = Z_LEAD // MIX_TR
    hpt = MIX_TR // POOL_HALO
    gate0 = 2 * D_MODEL // MIX_GATE_TN
    tcat, masks = _gla_constants()
    bands, hbands = _pool_constants()
    const2 = lambda i: (0, 0)
    const3 = lambda i: (0, 0, 0)
    return pl.pallas_call(
        _mixer_kernel,
        grid=(n,),
        in_specs=[
            pl.BlockSpec((MIX_TR, D_POOL), lambda i: (lead + i, OFF_U // D_POOL)),
            pl.BlockSpec((POOL_HALO, D_POOL), lambda i: ((lead + i) * hpt - 1, OFF_U // D_POOL)),
            pl.BlockSpec((MIX_TR, D_POOL), lambda i: (lead + i, OFF_GP // D_POOL)),
            pl.BlockSpec((MIX_TR, GLA_KEY), lambda i: (lead + i, OFF_Q // GLA_KEY)),
            pl.BlockSpec((MIX_TR, GLA_KEY), lambda i: (lead + i, OFF_K // GLA_KEY)),
            pl.BlockSpec((MIX_TR, D_GLA), lambda i: (lead + i, OFF_V // D_GLA)),
            pl.BlockSpec((MIX_TR, D_GLA), lambda i: (lead + i, OFF_GG // D_GLA)),
            pl.BlockSpec((MIX_TR, RANK_PAD), lambda i: (i, 0)),
            pl.BlockSpec(w_pool_bf.shape, const3),
            pl.BlockSpec((1, D_POOL), const2),
            pl.BlockSpec(bands.shape, const3),
            pl.BlockSpec(hbands.shape, const3),
            pl.BlockSpec((RANK_PAD, GLA_KEY), const2),
            pl.BlockSpec((1, GLA_KEY), const2),
            pl.BlockSpec((1, GLA_DV), const2),
            pl.BlockSpec(tcat.shape, const2),
            pl.BlockSpec(masks.shape, const3),
            pl.BlockSpec((D_MODEL, 1), const2),
            pl.BlockSpec((D_MODEL, MIX_GATE_TN), lambda i: (0, gate0 + i)),
            pl.BlockSpec((1, MIX_GATE_TN), lambda i: (0, gate0 + i)),
        ],
        out_specs=[
            pl.BlockSpec((MIX_TR, D_POOL + D_GLA), lambda i: (i, 0)),
            pl.BlockSpec((1, MIX_GATE_TN), lambda i: (0, i)),
        ],
        out_shape=[
            jax.ShapeDtypeStruct((t, D_POOL + D_GLA), BF16),
            jax.ShapeDtypeStruct((1, D_MODEL), F32),
        ],
        scratch_shapes=[pltpu.VMEM((GLA_HEADS, GLA_DV, GLA_DK), F32)],
        compiler_params=_params(1),
        name="mixer",
    )(z, z, z, z, z, z, z, a_lr, w_pool_bf, pool_scale, bands, hbands, w_alpha_pad, b_alpha,
      gla_norm_w, tcat, masks, c_act, w_ada, b_ada)


OUT_TM = 256
OUT_NC = 512
OUT_VMEM_LIMIT = 60 * 1024 * 1024


def _outproj_kernel(y_ref, w_ref, x_ref, gate_ref, fw_ref, o_ref):
    ym = y_ref[...]
    ssq = jnp.zeros((OUT_TM, 1), F32)
    for c in range(D_MODEL // OUT_NC):
        cs = slice(c * OUT_NC, (c + 1) * OUT_NC)
        y = _dot(ym, w_ref[:, cs])
        xn = x_ref[:, cs] + gate_ref[:, cs] * y
        o_ref[:, cs] = xn
        ssq = ssq + jnp.sum(xn * xn, axis=-1, keepdims=True)
    inv = lax.rsqrt(ssq * (1.0 / D_MODEL) + EPS)
    o_ref[...] = o_ref[...] * inv * fw_ref[...]


def _outproj(y, w_out_bf, x2, gate, final_norm_w):
    t = x2.shape[0]
    return pl.pallas_call(
        _outproj_kernel,
        grid=(t // OUT_TM,),
        in_specs=[
            pl.BlockSpec((OUT_TM, D_POOL + D_GLA), lambda i: (i, 0)),
            pl.BlockSpec((D_MODEL, D_MODEL), lambda i: (0, 0), pipeline_mode=pl.Buffered(1)),
            pl.BlockSpec((OUT_TM, D_MODEL), lambda i: (i, 0)),
            pl.BlockSpec((1, D_MODEL), lambda i: (0, 0)),
            pl.BlockSpec((1, D_MODEL), lambda i: (0, 0)),
        ],
        out_specs=pl.BlockSpec((OUT_TM, D_MODEL), lambda i: (i, 0)),
        out_shape=jax.ShapeDtypeStruct((t, D_MODEL), F32),
        compiler_params=_params(1, OUT_VMEM_LIMIT),
        name="outproj",
    )(y, w_out_bf, x2, gate, final_norm_w)


def kernel(x, c, w_ada, b_ada, norm_w, w_in, w_pool, pool_scale, w_alpha, b_alpha,
           gla_norm_w, w_out, final_norm_w):
    bsz, t, d = x.shape
    assert bsz == 1 and d == D_MODEL and w_ada.shape[0] == 1
    x2 = x.reshape(t, d)

    w_in_t = w_in[0].T
    mod, c_act, w_main = _prelude(c.reshape(d, 1), w_ada[0], b_ada, 2 * d, w_in_t)
    w_a = jnp.pad(jnp.tile(w_in_t[D_MAIN:], (3, 1)), ((0, RANK_PAD - 3 * GLA_RANK), (0, 0))).astype(BF16)
    z, a_lr, w_out_bf = _inproj(x2, norm_w, mod, w_a, w_main, w_out[0])

    w_alpha_pad = jnp.pad(jnp.tile(w_alpha[0], (3, 1)), ((0, RANK_PAD - 3 * GLA_RANK), (0, 0)))
    y, gate = _mixer(z, a_lr, w_pool[0].astype(BF16), pool_scale, w_alpha_pad, b_alpha, gla_norm_w,
                     c_act, w_ada[0], b_ada)

    out = _outproj(y, w_out_bf, x2, gate, final_norm_w.reshape(1, d))
    return out.reshape(bsz, t, d).astype(x.dtype)
```

```python
import math

import jax
import jax.numpy as jnp
import numpy as np
from jax import lax
from jax.experimental import pallas as pl
from jax.experimental.pallas import tpu as pltpu

F32 = jnp.float32
BF16 = jnp.bfloat16

D_MODEL = 4096
D_POOL = 2048
POOL_WINDOWS = (2, 4, 8, 16)
POOL_GROUP = 512
POOL_HALO = 16
D_GLA = 2048
GLA_HEADS = 4
GLA_DV = 512
GLA_DK = 256
GLA_KEY = 1024
GLA_RANK = 16
GLA_TAU = 16.0
EPS = 1e-6
D_MAIN = 2 * D_POOL + 2 * GLA_KEY + 2 * D_GLA
RANK_PAD = 128
LOG2E = math.log2(math.e)

OFF_U, OFF_GP, OFF_Q, OFF_K, OFF_V, OFF_GG = 0, 2048, 4096, 5120, 6144, 8192

VMEM_LIMIT = 56 * 1024 * 1024


def _dot(a, b):
    return jnp.dot(a, b, preferred_element_type=F32)


def _dot_nt(a, b):
    return lax.dot_general(a, b, (((1,), (1,)), ((), ())), preferred_element_type=F32)


def _dot_tn(a, b):
    return lax.dot_general(a, b, (((0,), (0,)), ((), ())), preferred_element_type=F32)


def _split_bf16(a):
    hi = a.astype(BF16)
    lo = (a - hi.astype(F32)).astype(BF16)
    return hi, lo


def _silu(v):
    return v * jax.nn.sigmoid(v)


def _params(n_axes, vmem_limit=VMEM_LIMIT):
    return pltpu.CompilerParams(
        dimension_semantics=("arbitrary",) * n_axes, vmem_limit_bytes=vmem_limit)


ADA_TN = 512
ADA_KC = 256
CAST_TM = 512


def _prelude_kernel(c_ref, w_ref, b_ref, wt_ref, wp_ref, ps_ref, o_ref, ca_ref, wbf_ref):
    tn = o_ref.shape[1]
    j = pl.program_id(0)

    @pl.when(j == 0)
    def _():
        ca_ref[...] = _silu(c_ref[...])

    @pl.when(j < D_POOL // CAST_TM)
    def _():
        folded = _dot_tn(wp_ref[0].astype(BF16), wt_ref[...].astype(BF16))
        wbf_ref[...] = (folded * ps_ref[...]).astype(wbf_ref.dtype)

    @pl.when(j >= D_POOL // CAST_TM)
    def _():
        wbf_ref[...] = wt_ref[...].astype(wbf_ref.dtype)

    def body(kk, acc):
        r = pl.multiple_of(kk * ADA_KC, ADA_KC)
        p = ca_ref[pl.ds(r, ADA_KC), :] * w_ref[pl.ds(r, ADA_KC), :]
        return acc + p.reshape(ADA_KC // 8, 8, tn).sum(axis=0)

    acc = lax.fori_loop(0, D_MODEL // ADA_KC, body, jnp.zeros((8, tn), F32))
    o_ref[...] = acc.sum(axis=0, keepdims=True) + b_ref[...]


def _prelude(c_col, w_ada, b_ada, n, w_t, w_pool, pool_scale_col):
    n_ada, n_cast = n // ADA_TN, D_MAIN // CAST_TM
    n_grp = D_POOL // CAST_TM
    assert n_cast >= n_ada and CAST_TM == POOL_GROUP and w_pool.shape[0] == n_grp
    ada_blk = lambda j: (0, jnp.minimum(j, n_ada - 1))
    return pl.pallas_call(
        _prelude_kernel,
        grid=(n_cast,),
        in_specs=[
            pl.BlockSpec((D_MODEL, 1), lambda j: (0, 0)),
            pl.BlockSpec((D_MODEL, ADA_TN), ada_blk),
            pl.BlockSpec((1, ADA_TN), ada_blk),
            pl.BlockSpec((CAST_TM, w_t.shape[1]), lambda j: (j, 0)),
            pl.BlockSpec((1, POOL_GROUP, POOL_GROUP), lambda j: (jnp.minimum(j, n_grp - 1), 0, 0)),
            pl.BlockSpec((POOL_GROUP, 1), lambda j: (jnp.minimum(j, n_grp - 1), 0)),
        ],
        out_specs=[
            pl.BlockSpec((1, ADA_TN), ada_blk),
            pl.BlockSpec((D_MODEL, 1), lambda j: (0, 0)),
            pl.BlockSpec((CAST_TM, w_t.shape[1]), lambda j: (j, 0)),
        ],
        out_shape=[
            jax.ShapeDtypeStruct((1, n), F32),
            jax.ShapeDtypeStruct((D_MODEL, 1), F32),
            jax.ShapeDtypeStruct((D_MAIN, w_t.shape[1]), BF16),
        ],
        compiler_params=_params(1),
        name="prelude",
    )(c_col, w_ada, b_ada, w_t, w_pool, pool_scale_col)


IN_TM = 1024
IN_TN = 1280
IN_SLAB = IN_TM // (D_MAIN // IN_TN)
IN_NSPLIT = 768
IN_WO_ROWS = 64
IN_VMEM_LIMIT = 60 * 1024 * 1024
Z_LEAD = IN_TM


def _inproj_step(h_src, h_dst, j, x_ref, nw_ref, shift_ref, scale_ref, wa_ref, w_ref, wo_ref,
                 z_ref, a_ref, wo_bf_ref):
    xs = x_ref[...]
    ms = jnp.mean(xs * xs, axis=-1, keepdims=True)
    mul = nw_ref[...] * (1.0 + scale_ref[...])
    h = (xs * lax.rsqrt(ms + EPS) * mul + shift_ref[...]).astype(BF16)
    h_dst[pl.ds(pl.multiple_of(j * IN_SLAB, IN_SLAB), IN_SLAB), :] = h
    if h_src is None:
        z_ref[...] = jnp.zeros_like(z_ref)
        a_ref[...] = _dot_nt(h, wa_ref[...])
    else:
        hs = h_src[...]
        z_ref[:, :IN_NSPLIT] = _dot_nt(hs, w_ref[:IN_NSPLIT, :]).astype(z_ref.dtype)
        a_ref[...] = _dot_nt(h, wa_ref[...])
        z_ref[:, IN_NSPLIT:] = _dot_nt(hs, w_ref[IN_NSPLIT:, :]).astype(z_ref.dtype)
    wo_bf_ref[...] = wo_ref[...].astype(wo_bf_ref.dtype)


def _inproj_kernel(x_ref, nw_ref, shift_ref, scale_ref, wa_ref, w_ref, wo_ref,
                   z_ref, a_ref, wo_bf_ref, h_even, h_odd):
    i = pl.program_id(0)
    j = pl.program_id(1)
    refs = (x_ref, nw_ref, shift_ref, scale_ref, wa_ref, w_ref, wo_ref, z_ref, a_ref, wo_bf_ref)
    pl.when(i == 0)(lambda: _inproj_step(None, h_even, j, *refs))
    pl.when((i & 1) == 1)(lambda: _inproj_step(h_even, h_odd, j, *refs))
    pl.when(jnp.logical_and(i > 0, (i & 1) == 0))(lambda: _inproj_step(h_odd, h_even, j, *refs))


def _inproj(x2, norm_w, mod, w_a, w_main, w_out):
    t = x2.shape[0]
    n_i, n_j = t // IN_TM, D_MAIN // IN_TN
    n_wo = w_out.shape[0] // IN_WO_ROWS
    assert IN_SLAB * n_j == IN_TM and (n_i + 1) * n_j >= n_wo

    def x_slab(i, j):
        return (jnp.minimum(i, n_i - 1) * n_j + j, 0)

    def wo_slab(i, j):
        return (jnp.minimum(i * n_j + j, n_wo - 1), 0)

    return pl.pallas_call(
        _inproj_kernel,
        grid=(n_i + 1, n_j),
        in_specs=[
            pl.BlockSpec((IN_SLAB, D_MODEL), x_slab),
            pl.BlockSpec((1, D_MODEL), lambda i, j: (0, 0)),
            pl.BlockSpec((1, D_MODEL), lambda i, j: (0, 0)),
            pl.BlockSpec((1, D_MODEL), lambda i, j: (0, 1)),
            pl.BlockSpec((RANK_PAD, D_MODEL), lambda i, j: (0, 0)),
            pl.BlockSpec((IN_TN, D_MODEL), lambda i, j: (jnp.where(i == 0, 0, j), 0)),
            pl.BlockSpec((IN_WO_ROWS, w_out.shape[1]), wo_slab),
        ],
        out_specs=[
            pl.BlockSpec((IN_TM, IN_TN), lambda i, j: (i, j)),
            pl.BlockSpec((IN_SLAB, RANK_PAD), lambda i, j: (i * n_j + j, 0)),
            pl.BlockSpec((IN_WO_ROWS, w_out.shape[1]), wo_slab),
        ],
        out_shape=[
            jax.ShapeDtypeStruct((Z_LEAD + t, D_MAIN), BF16),
            jax.ShapeDtypeStruct((t + IN_TM, RANK_PAD), F32),
            jax.ShapeDtypeStruct(w_out.shape, BF16),
        ],
        scratch_shapes=[pltpu.VMEM((IN_TM, D_MODEL), BF16), pltpu.VMEM((IN_TM, D_MODEL), BF16)],
        compiler_params=_params(2, IN_VMEM_LIMIT),
        name="inproj",
    )(x2, norm_w, mod, mod, w_a, w_main, w_out)


MIX_TR = 256


def _pool_constants():
    r = np.arange(MIX_TR)[:, None]
    c = np.arange(MIX_TR)[None, :]
    hr = np.arange(POOL_HALO)[:, None]
    hc = np.arange(POOL_HALO)[None, :]
    bands, hbands = [], []
    for w in POOL_WINDOWS:
        bands.append(np.where((c <= r) & (c > r - w), 1.0 / w, 0.0) - (r == c))
        hbands.append(np.where(hr + POOL_HALO - hc < w, 1.0 / w, 0.0))
    return (jnp.asarray(np.stack(bands), BF16), jnp.asarray(np.stack(hbands), BF16))


def _pool_tile(i, u_ref, up_ref, gp_ref, band_ref, hband_ref, y_ref):
    pad = jnp.zeros((MIX_TR - POOL_HALO, POOL_GROUP), F32)
    rows = lax.broadcasted_iota(jnp.int32, (POOL_HALO, 1), 0)
    for g, w in enumerate(POOL_WINDOWS):
        cs = slice(g * POOL_GROUP, (g + 1) * POOL_GROUP)
        u = u_ref[:, cs]
        pooled = _dot(band_ref[g], u) + jnp.concatenate([_dot(hband_ref[g], up_ref[:, cs]), pad], axis=0)
        ratio = jnp.where(i == 0, float(w) / jnp.minimum(rows + 1, w).astype(F32), 1.0)
        u_head = u[:POOL_HALO].astype(F32)
        head = ratio * (pooled[:POOL_HALO] + u_head) - u_head
        pooled = jnp.concatenate([head, pooled[POOL_HALO:]], axis=0)
        y_ref[:, cs] = pooled.astype(BF16) * _silu(gp_ref[:, cs])


GLA_TR = 256
GLA_LEVELS = GLA_TR.bit_length() - 1
GLA_MXU_LEVELS = 4


def _gla_constants():
    r = np.arange(GLA_TR)[:, None]
    t = np.arange(GLA_TR)[None, :]
    mats = []
    for k in range(GLA_MXU_LEVELS):
        half = ((r >> (k + 1)) << (k + 1)) + (1 << k)
        second = ((r >> k) & 1) == 1
        mats.append(np.where(second, (t >= half) & (t <= r), (t > r) & (t < half)))
    mats.append(t <= r)
    tcat = np.concatenate(mats, axis=0).astype(np.float32)
    diff = np.maximum(r ^ t, 1)
    level = np.where(t == r, GLA_LEVELS, np.where(t < r, np.floor(np.log2(diff)).astype(np.int64), -1))
    masks = np.stack([(level == k) for k in range(GLA_LEVELS + 1)]).astype(np.float32) * GLA_DK ** -0.5
    return jnp.asarray(tcat, BF16), jnp.asarray(masks, BF16)


def _gla_head(q_bf, k_bf, v, gg, g2, nw, tcat, mk_ref, st_ref, hd):
    tr, nl = GLA_TR, GLA_LEVELS
    g_hi, g_lo = _split_bf16(g2)
    n_small = GLA_MXU_LEVELS * tr
    dall = _dot(tcat[:n_small], g_hi)
    b2 = _dot(tcat[n_small:], g_hi) + _dot(tcat[n_small:], g_lo)
    rbit = lax.broadcasted_iota(jnp.int32, (tr, 1), 0)

    a_bf = mk_ref[nl] * _dot_nt(q_bf, k_bf).astype(BF16)
    for k in range(nl):
        if k < GLA_MXU_LEVELS:
            side = jnp.where(((rbit >> k) & 1) == 1, q_bf, k_bf)
            m = side * jnp.exp2(dall[k * tr:(k + 1) * tr]).astype(BF16)
        else:
            s = 1 << k
            parts = []
            for j in range(tr // (2 * s)):
                lo = 2 * s * j
                bp = b2[lo + s - 1:lo + s]
                parts.append(k_bf[lo:lo + s] * jnp.exp2(bp - b2[lo:lo + s]).astype(BF16))
                parts.append(q_bf[lo + s:lo + 2 * s] * jnp.exp2(b2[lo + s:lo + 2 * s] - bp).astype(BF16))
            m = jnp.concatenate(parts, axis=0)
        a_bf = a_bf + mk_ref[k] * _dot_nt(m, m).astype(BF16)

    b_end = b2[tr - 1:tr]
    q_in = q_bf * jnp.exp2(b2).astype(BF16)
    k_out = k_bf * jnp.exp2((b_end - 0.5 * math.log2(GLA_DK)) - b2).astype(BF16)
    st = st_ref[hd]
    o = _dot(a_bf, v) + _dot_nt(q_in, st.astype(BF16))
    st_ref[hd] = jnp.exp2(b_end) * st + _dot_tn(v, k_out)

    ms = jnp.mean(o * o, axis=-1, keepdims=True)
    on = (o * lax.rsqrt(ms + EPS)).astype(BF16)
    return on * (_silu(gg) * nw)


MIX_GATE_TN = 128
MIX_GATE_KC = 256


def _gate_block(c_ref, wg_ref, bg_ref, gate_ref):
    acc = jnp.zeros((8, MIX_GATE_TN), F32)
    for kk in range(D_MODEL // MIX_GATE_KC):
        rs = slice(kk * MIX_GATE_KC, (kk + 1) * MIX_GATE_KC)
        p = c_ref[rs, :] * wg_ref[rs, :]
        acc = acc + p.reshape(MIX_GATE_KC // 8, 8, MIX_GATE_TN).sum(axis=0)
    gate_ref[...] = acc.sum(axis=0, keepdims=True) + bg_ref[...]


def _mixer_kernel(u_ref, up_ref, gp_ref, q_ref, k_ref, v_ref, gg_ref, a_ref,
                  band_ref, hband_ref, wa_ref, ba_ref, nw_ref, tc_ref, mk_ref,
                  c_ref, wg_ref, bg_ref, y_ref, gate_ref, st_ref):
    i = pl.program_id(0)

    @pl.when(i == 0)
    def _():
        st_ref[...] = jnp.zeros_like(st_ref)

    _gate_block(c_ref, wg_ref, bg_ref, gate_ref)
    _pool_tile(i, u_ref, up_ref, gp_ref, band_ref, hband_ref, y_ref)

    a = a_ref[...]
    a_hi = a.astype(BF16).astype(F32)
    lane = lax.broadcasted_iota(jnp.int32, a.shape, 1)
    a3 = jnp.where((lane >= GLA_RANK) & (lane < 2 * GLA_RANK), a - a_hi, a_hi).astype(BF16)
    w = wa_ref[...]
    w_hi = w.astype(BF16).astype(F32)
    wrow = lax.broadcasted_iota(jnp.int32, w.shape, 0)
    w3 = jnp.where(wrow >= 2 * GLA_RANK, w - w_hi, w_hi).astype(BF16)
    xl = (_dot(a3, w3) + ba_ref[...]) * LOG2E
    g2 = (jnp.minimum(xl, 0.0) - jnp.log2(1.0 + jnp.exp2(-jnp.abs(xl)))) * (1.0 / GLA_TAU)
    tcat = tc_ref[...]
    nw = nw_ref[...].astype(BF16)
    for hd in range(GLA_HEADS):
        ks = slice(hd * GLA_DK, (hd + 1) * GLA_DK)
        vs = slice(hd * GLA_DV, (hd + 1) * GLA_DV)
        ys = slice(D_POOL + hd * GLA_DV, D_POOL + (hd + 1) * GLA_DV)
        y_ref[:, ys] = _gla_head(q_ref[:, ks], k_ref[:, ks], v_ref[:, vs], gg_ref[:, vs],
                                 g2[:, ks], nw, tcat, mk_ref, st_ref, hd)


def _mixer(z, a_lr, w_alpha_pad, b_alpha, gla_norm_w, c_act, w_ada, b_ada):
    t = z.shape[0] - Z_LEAD
    n = t // MIX_TR
    assert MIX_TR == GLA_TR and n * MIX_GATE_TN == D_MODEL
    lead = Z_LEAD // MIX_TR
    hpt = MIX_TR // POOL_HALO
    gate0 = 2 * D_MODEL // MIX_GATE_TN
    tcat, masks = _gla_constants()
    bands, hbands = _pool_constants()
    const2 = lambda i: (0, 0)
    const3 = lambda i: (0, 0, 0)
    return pl.pallas_call(
        _mixer_kernel,
        grid=(n,),
        in_specs=[
            pl.BlockSpec((MIX_TR, D_POOL), lambda i: (lead + i, OFF_U // D_POOL)),
            pl.BlockSpec((POOL_HALO, D_POOL), lambda i: ((lead + i) * hpt - 1, OFF_U // D_POOL)),
            pl.BlockSpec((MIX_TR, D_POOL), lambda i: (lead + i, OFF_GP // D_POOL)),
            pl.BlockSpec((MIX_TR, GLA_KEY), lambda i: (lead + i, OFF_Q // GLA_KEY)),
            pl.BlockSpec((MIX_TR, GLA_KEY), lambda i: (lead + i, OFF_K // GLA_KEY)),
            pl.BlockSpec((MIX_TR, D_GLA), lambda i: (lead + i, OFF_V // D_GLA)),
            pl.BlockSpec((MIX_TR, D_GLA), lambda i: (lead + i, OFF_GG // D_GLA)),
            pl.BlockSpec((MIX_TR, RANK_PAD), lambda i: (i, 0)),
            pl.BlockSpec(bands.shape, const3),
            pl.BlockSpec(hbands.shape, const3),
            pl.BlockSpec((RANK_PAD, GLA_KEY), const2),
            pl.BlockSpec((1, GLA_KEY), const2),
            pl.BlockSpec((1, GLA_DV), const2),
            pl.BlockSpec(tcat.shape, const2),
            pl.BlockSpec(masks.shape, const3),
            pl.BlockSpec((D_MODEL, 1), const2),
            pl.BlockSpec((D_MODEL, MIX_GATE_TN), lambda i: (0, gate0 + i)),
            pl.BlockSpec((1, MIX_GATE_TN), lambda i: (0, gate0 + i)),
        ],
        out_specs=[
            pl.BlockSpec((MIX_TR, D_POOL + D_GLA), lambda i: (i, 0)),
            pl.BlockSpec((1, MIX_GATE_TN), lambda i: (0, i)),
        ],
        out_shape=[
            jax.ShapeDtypeStruct((t, D_POOL + D_GLA), BF16),
            jax.ShapeDtypeStruct((1, D_MODEL), F32),
        ],
        scratch_shapes=[pltpu.VMEM((GLA_HEADS, GLA_DV, GLA_DK), F32)],
        compiler_params=_params(1),
        name="mixer",
    )(z, z, z, z, z, z, z, a_lr, bands, hbands, w_alpha_pad, b_alpha,
      gla_norm_w, tcat, masks, c_act, w_ada, b_ada)


OUT_TM = 256
OUT_NC = 512
OUT_VMEM_LIMIT = 63 * 1024 * 1024


def _outproj_kernel(y_ref, w_ref, x_ref, gate_ref, fw_ref, o_ref, xn_scr, inv_scr):
    i = pl.program_id(0)
    last = pl.num_programs(0) - 1
    chunks = [slice(c * OUT_NC, (c + 1) * OUT_NC) for c in range(D_MODEL // OUT_NC)]

    @pl.when(i == 0)
    def _():
        xn_scr[...] = jnp.zeros_like(xn_scr)
        inv_scr[...] = jnp.zeros_like(inv_scr)

    @pl.when(i < last)
    def _():
        inv_prev = inv_scr[...]
        ym = y_ref[...]
        ssq = jnp.zeros((OUT_TM, 1), F32)
        for cs in chunks:
            o_ref[:, cs] = xn_scr[:, cs] * inv_prev * fw_ref[:, cs]
            xn = x_ref[:, cs] + gate_ref[:, cs] * _dot(ym, w_ref[:, cs])
            xn_scr[:, cs] = xn
            ssq = ssq + jnp.sum(xn * xn, axis=-1, keepdims=True)
        inv_scr[...] = lax.rsqrt(ssq * (1.0 / D_MODEL) + EPS)

    @pl.when(i == last)
    def _():
        o_ref[...] = xn_scr[...] * inv_scr[...] * fw_ref[...]


def _outproj(y, w_out_bf, x2, gate, final_norm_w):
    t = x2.shape[0]
    n = t // OUT_TM
    cur = lambda i: (jnp.minimum(i, n - 1), 0)
    return pl.pallas_call(
        _outproj_kernel,
        grid=(n + 1,),
        in_specs=[
            pl.BlockSpec((OUT_TM, D_POOL + D_GLA), cur),
            pl.BlockSpec((D_MODEL, D_MODEL), lambda i: (0, 0), pipeline_mode=pl.Buffered(1)),
            pl.BlockSpec((OUT_TM, D_MODEL), cur),
            pl.BlockSpec((1, D_MODEL), lambda i: (0, 0)),
            pl.BlockSpec((1, D_MODEL), lambda i: (0, 0)),
        ],
        out_specs=pl.BlockSpec((OUT_TM, D_MODEL), lambda i: (jnp.maximum(i - 1, 0), 0)),
        out_shape=jax.ShapeDtypeStruct((t, D_MODEL), F32),
        scratch_shapes=[pltpu.VMEM((OUT_TM, D_MODEL), F32), pltpu.VMEM((OUT_TM, 1), F32)],
        compiler_params=_params(1, OUT_VMEM_LIMIT),
        name="outproj",
    )(y, w_out_bf, x2, gate, final_norm_w)


def kernel(x, c, w_ada, b_ada, norm_w, w_in, w_pool, pool_scale, w_alpha, b_alpha,
           gla_norm_w, w_out, final_norm_w):
    bsz, t, d = x.shape
    assert bsz == 1 and d == D_MODEL and w_ada.shape[0] == 1
    x2 = x.reshape(t, d)

    w_in_t = w_in[0].T
    mod, c_act, w_main = _prelude(c.reshape(d, 1), w_ada[0], b_ada, 2 * d, w_in_t,
                                  w_pool[0], pool_scale.reshape(D_POOL, 1))
    w_a = jnp.pad(jnp.tile(w_in_t[D_MAIN:], (3, 1)), ((0, RANK_PAD - 3 * GLA_RANK), (0, 0))).astype(BF16)
    z, a_lr, w_out_bf = _inproj(x2, norm_w, mod, w_a, w_main, w_out[0])

    w_alpha_pad = jnp.pad(jnp.tile(w_alpha[0], (3, 1)), ((0, RANK_PAD - 3 * GLA_RANK), (0, 0)))
    y, gate = _mixer(z, a_lr, w_alpha_pad, b_alpha, gla_norm_w,
                     c_act, w_ada[0], b_ada)

    out = _outproj(y, w_out_bf, x2, gate, final_norm_w.reshape(1, d))
    return out.reshape(bsz, t, d).astype(x.dtype)
```

```python
import math

import jax
import jax.numpy as jnp
import numpy as np
from jax import lax
from jax.experimental import pallas as pl
from jax.experimental.pallas import tpu as pltpu

F32 = jnp.float32
BF16 = jnp.bfloat16

D_MODEL = 4096
D_POOL = 2048
POOL_WINDOWS = (2, 4, 8, 16)
POOL_GROUP = 512
POOL_HALO = 16
D_GLA = 2048
GLA_HEADS = 4
GLA_DV = 512
GLA_DK = 256
GLA_KEY = 1024
GLA_RANK = 16
GLA_TAU = 16.0
EPS = 1e-6
D_MAIN = 2 * D_POOL + 2 * GLA_KEY + 2 * D_GLA
RANK_PAD = 128
LOG2E = math.log2(math.e)

OFF_U, OFF_GP, OFF_Q, OFF_K, OFF_V, OFF_GG = 0, 2048, 4096, 5120, 6144, 8192

VMEM_LIMIT = 56 * 1024 * 1024


def _dot(a, b):
    return jnp.dot(a, b, preferred_element_type=F32)


def _dot_nt(a, b):
    return lax.dot_general(a, b, (((1,), (1,)), ((), ())), preferred_element_type=F32)


def _dot_tn(a, b):
    return lax.dot_general(a, b, (((0,), (0,)), ((), ())), preferred_element_type=F32)


def _split_bf16(a):
    hi = a.astype(BF16)
    lo = (a - hi.astype(F32)).astype(BF16)
    return hi, lo


def _silu(v):
    return v * jax.nn.sigmoid(v)


def _params(n_axes, vmem_limit=VMEM_LIMIT):
    return pltpu.CompilerParams(
        dimension_semantics=("arbitrary",) * n_axes, vmem_limit_bytes=vmem_limit)


ADA_TN = 512
ADA_KC = 256
CAST_TM = 512


def _prelude_kernel(c_ref, w_ref, b_ref, wt_ref, wp_ref, ps_ref, o_ref, ca_ref, wbf_ref):
    tn = o_ref.shape[1]
    j = pl.program_id(0)

    @pl.when(j == 0)
    def _():
        ca_ref[...] = _silu(c_ref[...])

    @pl.when(j < D_POOL // CAST_TM)
    def _():
        folded = _dot_tn(wp_ref[0].astype(BF16), wt_ref[...].astype(BF16))
        wbf_ref[...] = (folded * ps_ref[...]).astype(wbf_ref.dtype)

    @pl.when(j >= D_POOL // CAST_TM)
    def _():
        wbf_ref[...] = wt_ref[...].astype(wbf_ref.dtype)

    def body(kk, acc):
        r = pl.multiple_of(kk * ADA_KC, ADA_KC)
        p = ca_ref[pl.ds(r, ADA_KC), :] * w_ref[pl.ds(r, ADA_KC), :]
        return acc + p.reshape(ADA_KC // 8, 8, tn).sum(axis=0)

    acc = lax.fori_loop(0, D_MODEL // ADA_KC, body, jnp.zeros((8, tn), F32))
    o_ref[...] = acc.sum(axis=0, keepdims=True) + b_ref[...]


def _prelude(c_col, w_ada, b_ada, n, w_t, w_pool, pool_scale_col):
    n_ada, n_cast = n // ADA_TN, D_MAIN // CAST_TM
    n_grp = D_POOL // CAST_TM
    assert n_cast >= n_ada and CAST_TM == POOL_GROUP and w_pool.shape[0] == n_grp
    ada_blk = lambda j: (0, jnp.minimum(j, n_ada - 1))
    return pl.pallas_call(
        _prelude_kernel,
        grid=(n_cast,),
        in_specs=[
            pl.BlockSpec((D_MODEL, 1), lambda j: (0, 0)),
            pl.BlockSpec((D_MODEL, ADA_TN), ada_blk),
            pl.BlockSpec((1, ADA_TN), ada_blk),
            pl.BlockSpec((CAST_TM, w_t.shape[1]), lambda j: (j, 0)),
            pl.BlockSpec((1, POOL_GROUP, POOL_GROUP), lambda j: (jnp.minimum(j, n_grp - 1), 0, 0)),
            pl.BlockSpec((POOL_GROUP, 1), lambda j: (jnp.minimum(j, n_grp - 1), 0)),
        ],
        out_specs=[
            pl.BlockSpec((1, ADA_TN), ada_blk),
            pl.BlockSpec((D_MODEL, 1), lambda j: (0, 0)),
            pl.BlockSpec((CAST_TM, w_t.shape[1]), lambda j: (j, 0)),
        ],
        out_shape=[
            jax.ShapeDtypeStruct((1, n), F32),
            jax.ShapeDtypeStruct((D_MODEL, 1), F32),
            jax.ShapeDtypeStruct((D_MAIN, w_t.shape[1]), BF16),
        ],
        compiler_params=_params(1),
        name="prelude",
    )(c_col, w_ada, b_ada, w_t, w_pool, pool_scale_col)


IN_TM = 1024
IN_TN = 1280
IN_SLAB = IN_TM // (D_MAIN // IN_TN)
IN_NSPLIT = 768
IN_WO_ROWS = 64
IN_VMEM_LIMIT = 60 * 1024 * 1024
Z_LEAD = IN_TM


def _inproj_step(h_src, h_dst, j, x_ref, nw_ref, shift_ref, scale_ref, wa_ref, w_ref, wo_ref,
                 z_ref, a_ref, wo_bf_ref):
    xs = x_ref[...]
    ms = jnp.mean(xs * xs, axis=-1, keepdims=True)
    mul = nw_ref[...] * (1.0 + scale_ref[...])
    h = (xs * lax.rsqrt(ms + EPS) * mul + shift_ref[...]).astype(BF16)
    h_dst[pl.ds(pl.multiple_of(j * IN_SLAB, IN_SLAB), IN_SLAB), :] = h
    if h_src is None:
        z_ref[...] = jnp.zeros_like(z_ref)
        a_ref[...] = _dot_nt(h, wa_ref[...])
    else:
        hs = h_src[...]
        z_ref[:, :IN_NSPLIT] = _dot_nt(hs, w_ref[:IN_NSPLIT, :]).astype(z_ref.dtype)
        a_ref[...] = _dot_nt(h, wa_ref[...])
        z_ref[:, IN_NSPLIT:] = _dot_nt(hs, w_ref[IN_NSPLIT:, :]).astype(z_ref.dtype)
    wo_bf_ref[...] = wo_ref[...].astype(wo_bf_ref.dtype)


def _inproj_kernel(x_ref, nw_ref, shift_ref, scale_ref, wa_ref, w_ref, wo_ref,
                   z_ref, a_ref, wo_bf_ref, h_even, h_odd):
    i = pl.program_id(0)
    j = pl.program_id(1)
    refs = (x_ref, nw_ref, shift_ref, scale_ref, wa_ref, w_ref, wo_ref, z_ref, a_ref, wo_bf_ref)
    pl.when(i == 0)(lambda: _inproj_step(None, h_even, j, *refs))
    pl.when((i & 1) == 1)(lambda: _inproj_step(h_even, h_odd, j, *refs))
    pl.when(jnp.logical_and(i > 0, (i & 1) == 0))(lambda: _inproj_step(h_odd, h_even, j, *refs))


def _inproj(x2, norm_w, mod, w_a, w_main, w_out):
    t = x2.shape[0]
    n_i, n_j = t // IN_TM, D_MAIN // IN_TN
    n_wo = w_out.shape[0] // IN_WO_ROWS
    assert IN_SLAB * n_j == IN_TM and (n_i + 1) * n_j >= n_wo

    def x_slab(i, j):
        return (jnp.minimum(i, n_i - 1) * n_j + j, 0)

    def wo_slab(i, j):
        return (jnp.minimum(i * n_j + j, n_wo - 1), 0)

    return pl.pallas_call(
        _inproj_kernel,
        grid=(n_i + 1, n_j),
        in_specs=[
            pl.BlockSpec((IN_SLAB, D_MODEL), x_slab),
            pl.BlockSpec((1, D_MODEL), lambda i, j: (0, 0)),
            pl.BlockSpec((1, D_MODEL), lambda i, j: (0, 0)),
            pl.BlockSpec((1, D_MODEL), lambda i, j: (0, 1)),
            pl.BlockSpec((RANK_PAD, D_MODEL), lambda i, j: (0, 0)),
            pl.BlockSpec((IN_TN, D_MODEL), lambda i, j: (jnp.where(i == 0, 0, j), 0)),
            pl.BlockSpec((IN_WO_ROWS, w_out.shape[1]), wo_slab),
        ],
        out_specs=[
            pl.BlockSpec((IN_TM, IN_TN), lambda i, j: (i, j)),
            pl.BlockSpec((IN_SLAB, RANK_PAD), lambda i, j: (i * n_j + j, 0)),
            pl.BlockSpec((IN_WO_ROWS, w_out.shape[1]), wo_slab),
        ],
        out_shape=[
            jax.ShapeDtypeStruct((Z_LEAD + t, D_MAIN), BF16),
            jax.ShapeDtypeStruct((t + IN_TM, RANK_PAD), F32),
            jax.ShapeDtypeStruct(w_out.shape, BF16),
        ],
        scratch_shapes=[pltpu.VMEM((IN_TM, D_MODEL), BF16), pltpu.VMEM((IN_TM, D_MODEL), BF16)],
        compiler_params=_params(2, IN_VMEM_LIMIT),
        name="inproj",
    )(x2, norm_w, mod, mod, w_a, w_main, w_out)


MIX_TR = 512


def _pool_constants():
    r = np.arange(GLA_TR)[:, None]
    c = np.arange(GLA_TR)[None, :]
    hr = np.arange(POOL_HALO)[:, None]
    hc = np.arange(POOL_HALO)[None, :]
    bands, hbands = [], []
    for w in POOL_WINDOWS:
        bands.append(np.where((c <= r) & (c > r - w), 1.0 / w, 0.0) - (r == c))
        hbands.append(np.where(hr + POOL_HALO - hc < w, 1.0 / w, 0.0))
    return (jnp.asarray(np.stack(bands), BF16), jnp.asarray(np.stack(hbands), BF16))


def _pool_tile(i, sub, u_ref, up_ref, gp_ref, band_ref, hband_ref, y_ref):
    r0 = sub * GLA_TR
    rs = slice(r0, r0 + GLA_TR)
    pad = jnp.zeros((GLA_TR - POOL_HALO, POOL_GROUP), F32)
    rows = lax.broadcasted_iota(jnp.int32, (POOL_HALO, 1), 0)
    for g, w in enumerate(POOL_WINDOWS):
        cs = slice(g * POOL_GROUP, (g + 1) * POOL_GROUP)
        u = u_ref[rs, cs]
        halo = up_ref[:, cs] if sub == 0 else u_ref[r0 - POOL_HALO:r0, cs]
        pooled = _dot(band_ref[g], u) + jnp.concatenate([_dot(hband_ref[g], halo), pad], axis=0)
        if sub == 0:
            ratio = jnp.where(i == 0, float(w) / jnp.minimum(rows + 1, w).astype(F32), 1.0)
            u_head = u[:POOL_HALO].astype(F32)
            head = ratio * (pooled[:POOL_HALO] + u_head) - u_head
            pooled = jnp.concatenate([head, pooled[POOL_HALO:]], axis=0)
        y_ref[rs, cs] = pooled.astype(BF16) * _silu(gp_ref[rs, cs])


GLA_TR = 256
GLA_LEVELS = GLA_TR.bit_length() - 1
GLA_MXU_LEVELS = 4


def _gla_constants():
    r = np.arange(GLA_TR)[:, None]
    t = np.arange(GLA_TR)[None, :]
    mats = []
    for k in range(GLA_MXU_LEVELS):
        half = ((r >> (k + 1)) << (k + 1)) + (1 << k)
        second = ((r >> k) & 1) == 1
        mats.append(np.where(second, (t >= half) & (t <= r), (t > r) & (t < half)))
    mats.append(t <= r)
    tcat = np.concatenate(mats, axis=0).astype(np.float32)
    diff = np.maximum(r ^ t, 1)
    level = np.where(t == r, GLA_LEVELS, np.where(t < r, np.floor(np.log2(diff)).astype(np.int64), -1))
    masks = np.stack([(level == k) for k in range(GLA_LEVELS + 1)]).astype(np.float32) * GLA_DK ** -0.5
    return jnp.asarray(tcat, BF16), jnp.asarray(masks, BF16)


def _gla_head(q_bf, k_bf, v, gg, g2, nw, tcat, mk_ref, st_ref, hd):
    tr, nl = GLA_TR, GLA_LEVELS
    g_hi, g_lo = _split_bf16(g2)
    n_small = GLA_MXU_LEVELS * tr
    dall = _dot(tcat[:n_small], g_hi)
    b2 = _dot(tcat[n_small:], g_hi) + _dot(tcat[n_small:], g_lo)
    rbit = lax.broadcasted_iota(jnp.int32, (tr, 1), 0)

    a_bf = mk_ref[nl] * _dot_nt(q_bf, k_bf).astype(BF16)
    for k in range(nl):
        if k < GLA_MXU_LEVELS:
            side = jnp.where(((rbit >> k) & 1) == 1, q_bf, k_bf)
            m = side * jnp.exp2(dall[k * tr:(k + 1) * tr]).astype(BF16)
        else:
            s = 1 << k
            parts = []
            for j in range(tr // (2 * s)):
                lo = 2 * s * j
                bp = b2[lo + s - 1:lo + s]
                parts.append(k_bf[lo:lo + s] * jnp.exp2(bp - b2[lo:lo + s]).astype(BF16))
                parts.append(q_bf[lo + s:lo + 2 * s] * jnp.exp2(b2[lo + s:lo + 2 * s] - bp).astype(BF16))
            m = jnp.concatenate(parts, axis=0)
        a_bf = a_bf + mk_ref[k] * _dot_nt(m, m).astype(BF16)

    b_end = b2[tr - 1:tr]
    q_in = q_bf * jnp.exp2(b2).astype(BF16)
    k_out = k_bf * jnp.exp2((b_end - 0.5 * math.log2(GLA_DK)) - b2).astype(BF16)
    st = st_ref[hd]
    o = _dot(a_bf, v) + _dot_nt(q_in, st.astype(BF16))
    st_ref[hd] = jnp.exp2(b_end) * st + _dot_tn(v, k_out)

    ms = jnp.mean(o * o, axis=-1, keepdims=True)
    on = (o * lax.rsqrt(ms + EPS)).astype(BF16)
    return on * (_silu(gg) * nw)


MIX_GATE_TN = 256
MIX_GATE_KC = 256


def _gate_block(c_ref, wg_ref, bg_ref, gate_ref):
    acc = jnp.zeros((8, MIX_GATE_TN), F32)
    for kk in range(D_MODEL // MIX_GATE_KC):
        rs = slice(kk * MIX_GATE_KC, (kk + 1) * MIX_GATE_KC)
        p = c_ref[rs, :] * wg_ref[rs, :]
        acc = acc + p.reshape(MIX_GATE_KC // 8, 8, MIX_GATE_TN).sum(axis=0)
    gate_ref[...] = acc.sum(axis=0, keepdims=True) + bg_ref[...]


def _mixer_kernel(u_ref, up_ref, gp_ref, q_ref, k_ref, v_ref, gg_ref, a_ref,
                  band_ref, hband_ref, wa_ref, ba_ref, nw_ref, tc_ref, mk_ref,
                  c_ref, wg_ref, bg_ref, y_ref, gate_ref, st_ref):
    i = pl.program_id(0)

    @pl.when(i == 0)
    def _():
        st_ref[...] = jnp.zeros_like(st_ref)

    _gate_block(c_ref, wg_ref, bg_ref, gate_ref)
    for sub in range(MIX_TR // GLA_TR):
        _pool_tile(i, sub, u_ref, up_ref, gp_ref, band_ref, hband_ref, y_ref)

    a = a_ref[...]
    a_hi = a.astype(BF16).astype(F32)
    lane = lax.broadcasted_iota(jnp.int32, a.shape, 1)
    a3 = jnp.where((lane >= GLA_RANK) & (lane < 2 * GLA_RANK), a - a_hi, a_hi).astype(BF16)
    w = wa_ref[...]
    w_hi = w.astype(BF16).astype(F32)
    wrow = lax.broadcasted_iota(jnp.int32, w.shape, 0)
    w3 = jnp.where(wrow >= 2 * GLA_RANK, w - w_hi, w_hi).astype(BF16)
    xl = (_dot(a3, w3) + ba_ref[...]) * LOG2E
    g2 = (jnp.minimum(xl, 0.0) - jnp.log2(1.0 + jnp.exp2(-jnp.abs(xl)))) * (1.0 / GLA_TAU)
    tcat = tc_ref[...]
    nw = nw_ref[...].astype(BF16)
    for sub in range(MIX_TR // GLA_TR):
        rs = slice(sub * GLA_TR, (sub + 1) * GLA_TR)
        for hd in range(GLA_HEADS):
            ks = slice(hd * GLA_DK, (hd + 1) * GLA_DK)
            vs = slice(hd * GLA_DV, (hd + 1) * GLA_DV)
            ys = slice(D_POOL + hd * GLA_DV, D_POOL + (hd + 1) * GLA_DV)
            y_ref[rs, ys] = _gla_head(q_ref[rs, ks], k_ref[rs, ks], v_ref[rs, vs], gg_ref[rs, vs],
                                      g2[rs, ks], nw, tcat, mk_ref, st_ref, hd)


def _mixer(z, a_lr, w_alpha_pad, b_alpha, gla_norm_w, c_act, w_ada, b_ada):
    t = z.shape[0] - Z_LEAD
    n = t // MIX_TR
    assert MIX_TR % GLA_TR == 0 and n * MIX_GATE_TN == D_MODEL
    lead = Z_LEAD // MIX_TR
    hpt = MIX_TR // POOL_HALO
    gate0 = 2 * D_MODEL // MIX_GATE_TN
    tcat, masks = _gla_constants()
    bands, hbands = _pool_constants()
    const2 = lambda i: (0, 0)
    const3 = lambda i: (0, 0, 0)
    return pl.pallas_call(
        _mixer_kernel,
        grid=(n,),
        in_specs=[
            pl.BlockSpec((MIX_TR, D_POOL), lambda i: (lead + i, OFF_U // D_POOL)),
            pl.BlockSpec((POOL_HALO, D_POOL), lambda i: ((lead + i) * hpt - 1, OFF_U // D_POOL)),
            pl.BlockSpec((MIX_TR, D_POOL), lambda i: (lead + i, OFF_GP // D_POOL)),
            pl.BlockSpec((MIX_TR, GLA_KEY), lambda i: (lead + i, OFF_Q // GLA_KEY)),
            pl.BlockSpec((MIX_TR, GLA_KEY), lambda i: (lead + i, OFF_K // GLA_KEY)),
            pl.BlockSpec((MIX_TR, D_GLA), lambda i: (lead + i, OFF_V // D_GLA)),
            pl.BlockSpec((MIX_TR, D_GLA), lambda i: (lead + i, OFF_GG // D_GLA)),
            pl.BlockSpec((MIX_TR, RANK_PAD), lambda i: (i, 0)),
            pl.BlockSpec(bands.shape, const3),
            pl.BlockSpec(hbands.shape, const3),
            pl.BlockSpec((RANK_PAD, GLA_KEY), const2),
            pl.BlockSpec((1, GLA_KEY), const2),
            pl.BlockSpec((1, GLA_DV), const2),
            pl.BlockSpec(tcat.shape, const2),
            pl.BlockSpec(masks.shape, const3),
            pl.BlockSpec((D_MODEL, 1), const2),
            pl.BlockSpec((D_MODEL, MIX_GATE_TN), lambda i: (0, gate0 + i)),
            pl.BlockSpec((1, MIX_GATE_TN), lambda i: (0, gate0 + i)),
        ],
        out_specs=[
            pl.BlockSpec((MIX_TR, D_POOL + D_GLA), lambda i: (i, 0)),
            pl.BlockSpec((1, MIX_GATE_TN), lambda i: (0, i)),
        ],
        out_shape=[
            jax.ShapeDtypeStruct((t, D_POOL + D_GLA), BF16),
            jax.ShapeDtypeStruct((1, D_MODEL), F32),
        ],
        scratch_shapes=[pltpu.VMEM((GLA_HEADS, GLA_DV, GLA_DK), F32)],
        compiler_params=_params(1),
        name="mixer",
    )(z, z, z, z, z, z, z, a_lr, bands, hbands, w_alpha_pad, b_alpha,
      gla_norm_w, tcat, masks, c_act, w_ada, b_ada)


OUT_TM = 256
OUT_NC = 512
OUT_VMEM_LIMIT = 63 * 1024 * 1024


def _outproj_kernel(y_ref, w_ref, x_ref, gate_ref, fw_ref, o_ref, xn_scr, inv_scr):
    i = pl.program_id(0)
    last = pl.num_programs(0) - 1
    chunks = [slice(c * OUT_NC, (c + 1) * OUT_NC) for c in range(D_MODEL // OUT_NC)]

    @pl.when(i == 0)
    def _():
        xn_scr[...] = jnp.zeros_like(xn_scr)
        inv_scr[...] = jnp.zeros_like(inv_scr)

    @pl.when(i < last)
    def _():
        inv_prev = inv_scr[...]
        ym = y_ref[...]
        ssq = jnp.zeros((OUT_TM, 1), F32)
        for cs in chunks:
            o_ref[:, cs] = xn_scr[:, cs] * inv_prev * fw_ref[:, cs]
            xn = x_ref[:, cs] + gate_ref[:, cs] * _dot(ym, w_ref[:, cs])
            xn_scr[:, cs] = xn
            ssq = ssq + jnp.sum(xn * xn, axis=-1, keepdims=True)
        inv_scr[...] = lax.rsqrt(ssq * (1.0 / D_MODEL) + EPS)

    @pl.when(i == last)
    def _():
        o_ref[...] = xn_scr[...] * inv_scr[...] * fw_ref[...]


def _outproj(y, w_out_bf, x2, gate, final_norm_w):
    t = x2.shape[0]
    n = t // OUT_TM
    cur = lambda i: (jnp.minimum(i, n - 1), 0)
    return pl.pallas_call(
        _outproj_kernel,
        grid=(n + 1,),
        in_specs=[
            pl.BlockSpec((OUT_TM, D_POOL + D_GLA), cur),
            pl.BlockSpec((D_MODEL, D_MODEL), lambda i: (0, 0), pipeline_mode=pl.Buffered(1)),
            pl.BlockSpec((OUT_TM, D_MODEL), cur),
            pl.BlockSpec((1, D_MODEL), lambda i: (0, 0)),
            pl.BlockSpec((1, D_MODEL), lambda i: (0, 0)),
        ],
        out_specs=pl.BlockSpec((OUT_TM, D_MODEL), lambda i: (jnp.maximum(i - 1, 0), 0)),
        out_shape=jax.ShapeDtypeStruct((t, D_MODEL), F32),
        scratch_shapes=[pltpu.VMEM((OUT_TM, D_MODEL), F32), pltpu.VMEM((OUT_TM, 1), F32)],
        compiler_params=_params(1, OUT_VMEM_LIMIT),
        name="outproj",
    )(y, w_out_bf, x2, gate, final_norm_w)


def kernel(x, c, w_ada, b_ada, norm_w, w_in, w_pool, pool_scale, w_alpha, b_alpha,
           gla_norm_w, w_out, final_norm_w):
    bsz, t, d = x.shape
    assert bsz == 1 and d == D_MODEL and w_ada.shape[0] == 1
    x2 = x.reshape(t, d)

    w_in_t = w_in[0].T
    mod, c_act, w_main = _prelude(c.reshape(d, 1), w_ada[0], b_ada, 2 * d, w_in_t,
                                  w_pool[0], pool_scale.reshape(D_POOL, 1))
    w_a = jnp.pad(jnp.tile(w_in_t[D_MAIN:], (3, 1)), ((0, RANK_PAD - 3 * GLA_RANK), (0, 0))).astype(BF16)
    z, a_lr, w_out_bf = _inproj(x2, norm_w, mod, w_a, w_main, w_out[0])

    w_alpha_pad = jnp.pad(jnp.tile(w_alpha[0], (3, 1)), ((0, RANK_PAD - 3 * GLA_RANK), (0, 0)))
    y, gate = _mixer(z, a_lr, w_alpha_pad, b_alpha, gla_norm_w,
                     c_act, w_ada[0], b_ada)

    out = _outproj(y, w_out_bf, x2, gate, final_norm_w.reshape(1, d))
    return out.reshape(bsz, t, d).astype(x.dtype)
```

```python
import math

import jax
import jax.numpy as jnp
import numpy as np
from jax import lax
from jax.experimental import pallas as pl
from jax.experimental.pallas import tpu as pltpu

F32 = jnp.float32
BF16 = jnp.bfloat16

D_MODEL = 4096
D_POOL = 2048
POOL_WINDOWS = (2, 4, 8, 16)
POOL_GROUP = 512
POOL_HALO = 16
D_GLA = 2048
GLA_HEADS = 4
GLA_DV = 512
GLA_DK = 256
GLA_KEY = 1024
GLA_RANK = 16
GLA_TAU = 16.0
EPS = 1e-6
D_MAIN = 2 * D_POOL + 2 * GLA_KEY + 2 * D_GLA
RANK_PAD = 128
LOG2E = math.log2(math.e)

OFF_U, OFF_GP, OFF_Q, OFF_K, OFF_V, OFF_GG = 0, 2048, 4096, 5120, 6144, 8192

VMEM_LIMIT = 56 * 1024 * 1024


def _dot(a, b):
    return jnp.dot(a, b, preferred_element_type=F32)


def _dot_nt(a, b):
    return lax.dot_general(a, b, (((1,), (1,)), ((), ())), preferred_element_type=F32)


def _dot_tn(a, b):
    return lax.dot_general(a, b, (((0,), (0,)), ((), ())), preferred_element_type=F32)


def _split_bf16(a):
    hi = a.astype(BF16)
    lo = (a - hi.astype(F32)).astype(BF16)
    return hi, lo


def _silu(v):
    return v * jax.nn.sigmoid(v)


def _params(n_axes, vmem_limit=VMEM_LIMIT):
    return pltpu.CompilerParams(
        dimension_semantics=("arbitrary",) * n_axes, vmem_limit_bytes=vmem_limit)


ADA_TN = 512
ADA_KC = 256
CAST_TM = 512


def _prelude_kernel(c_ref, w_ref, b_ref, wt_ref, wp_ref, ps_ref, o_ref, ca_ref, wbf_ref):
    tn = o_ref.shape[1]
    j = pl.program_id(0)

    @pl.when(j == 0)
    def _():
        ca_ref[...] = _silu(c_ref[...])

    @pl.when(j < D_POOL // CAST_TM)
    def _():
        folded = _dot_tn(wp_ref[0].astype(BF16), wt_ref[...].astype(BF16))
        wbf_ref[...] = (folded * ps_ref[...]).astype(wbf_ref.dtype)

    @pl.when(j >= D_POOL // CAST_TM)
    def _():
        wbf_ref[...] = wt_ref[...].astype(wbf_ref.dtype)

    def body(kk, acc):
        r = pl.multiple_of(kk * ADA_KC, ADA_KC)
        p = ca_ref[pl.ds(r, ADA_KC), :] * w_ref[pl.ds(r, ADA_KC), :]
        return acc + p.reshape(ADA_KC // 8, 8, tn).sum(axis=0)

    acc = lax.fori_loop(0, D_MODEL // ADA_KC, body, jnp.zeros((8, tn), F32))
    o_ref[...] = acc.sum(axis=0, keepdims=True) + b_ref[...]


def _prelude(c_col, w_ada, b_ada, n, w_t, w_pool, pool_scale_col):
    n_ada, n_cast = n // ADA_TN, D_MAIN // CAST_TM
    n_grp = D_POOL // CAST_TM
    assert n_cast >= n_ada and CAST_TM == POOL_GROUP and w_pool.shape[0] == n_grp
    ada_blk = lambda j: (0, jnp.minimum(j, n_ada - 1))
    return pl.pallas_call(
        _prelude_kernel,
        grid=(n_cast,),
        in_specs=[
            pl.BlockSpec((D_MODEL, 1), lambda j: (0, 0)),
            pl.BlockSpec((D_MODEL, ADA_TN), ada_blk),
            pl.BlockSpec((1, ADA_TN), ada_blk),
            pl.BlockSpec((CAST_TM, w_t.shape[1]), lambda j: (j, 0)),
            pl.BlockSpec((1, POOL_GROUP, POOL_GROUP), lambda j: (jnp.minimum(j, n_grp - 1), 0, 0)),
            pl.BlockSpec((POOL_GROUP, 1), lambda j: (jnp.minimum(j, n_grp - 1), 0)),
        ],
        out_specs=[
            pl.BlockSpec((1, ADA_TN), ada_blk),
            pl.BlockSpec((D_MODEL, 1), lambda j: (0, 0)),
            pl.BlockSpec((CAST_TM, w_t.shape[1]), lambda j: (j, 0)),
        ],
        out_shape=[
            jax.ShapeDtypeStruct((1, n), F32),
            jax.ShapeDtypeStruct((D_MODEL, 1), F32),
            jax.ShapeDtypeStruct((D_MAIN, w_t.shape[1]), BF16),
        ],
        compiler_params=_params(1),
        name="prelude",
    )(c_col, w_ada, b_ada, w_t, w_pool, pool_scale_col)


IN_TM = 1024
IN_TN = 1280
IN_SLAB = IN_TM // (D_MAIN // IN_TN)
IN_NSPLIT = 768
IN_WO_ROWS = 64
IN_VMEM_LIMIT = 60 * 1024 * 1024
Z_LEAD = IN_TM


def _inproj_step(h_src, h_dst, j, x_ref, nw_ref, shift_ref, scale_ref, wa_ref, w_ref, wo_ref,
                 z_ref, a_ref, wo_bf_ref):
    xs = x_ref[...]
    ms = jnp.mean(xs * xs, axis=-1, keepdims=True)
    mul = nw_ref[...] * (1.0 + scale_ref[...])
    h = (xs * lax.rsqrt(ms + EPS) * mul + shift_ref[...]).astype(BF16)
    h_dst[pl.ds(pl.multiple_of(j * IN_SLAB, IN_SLAB), IN_SLAB), :] = h
    if h_src is None:
        z_ref[...] = jnp.zeros_like(z_ref)
        a_ref[...] = _dot_nt(h, wa_ref[...])
    else:
        hs = h_src[...]
        z_ref[:, :IN_NSPLIT] = _dot_nt(hs, w_ref[:IN_NSPLIT, :]).astype(z_ref.dtype)
        a_ref[...] = _dot_nt(h, wa_ref[...])
        z_ref[:, IN_NSPLIT:] = _dot_nt(hs, w_ref[IN_NSPLIT:, :]).astype(z_ref.dtype)
    wo_bf_ref[...] = wo_ref[...].astype(wo_bf_ref.dtype)


def _inproj_kernel(x_ref, nw_ref, shift_ref, scale_ref, wa_ref, w_ref, wo_ref,
                   z_ref, a_ref, wo_bf_ref, h_even, h_odd):
    i = pl.program_id(0)
    j = pl.program_id(1)
    refs = (x_ref, nw_ref, shift_ref, scale_ref, wa_ref, w_ref, wo_ref, z_ref, a_ref, wo_bf_ref)
    pl.when(i == 0)(lambda: _inproj_step(None, h_even, j, *refs))
    pl.when((i & 1) == 1)(lambda: _inproj_step(h_even, h_odd, j, *refs))
    pl.when(jnp.logical_and(i > 0, (i & 1) == 0))(lambda: _inproj_step(h_odd, h_even, j, *refs))


def _inproj(x2, norm_w, mod, w_a, w_main, w_out):
    t = x2.shape[0]
    n_i, n_j = t // IN_TM, D_MAIN // IN_TN
    n_wo = w_out.shape[0] // IN_WO_ROWS
    assert IN_SLAB * n_j == IN_TM and (n_i + 1) * n_j >= n_wo

    def x_slab(i, j):
        return (jnp.minimum(i, n_i - 1) * n_j + j, 0)

    def wo_slab(i, j):
        return (jnp.minimum(i * n_j + j, n_wo - 1), 0)

    return pl.pallas_call(
        _inproj_kernel,
        grid=(n_i + 1, n_j),
        in_specs=[
            pl.BlockSpec((IN_SLAB, D_MODEL), x_slab),
            pl.BlockSpec((1, D_MODEL), lambda i, j: (0, 0)),
            pl.BlockSpec((1, D_MODEL), lambda i, j: (0, 0)),
            pl.BlockSpec((1, D_MODEL), lambda i, j: (0, 1)),
            pl.BlockSpec((RANK_PAD, D_MODEL), lambda i, j: (0, 0)),
            pl.BlockSpec((IN_TN, D_MODEL), lambda i, j: (jnp.where(i == 0, 0, j), 0)),
            pl.BlockSpec((IN_WO_ROWS, w_out.shape[1]), wo_slab),
        ],
        out_specs=[
            pl.BlockSpec((IN_TM, IN_TN), lambda i, j: (i, j)),
            pl.BlockSpec((IN_SLAB, RANK_PAD), lambda i, j: (i * n_j + j, 0)),
            pl.BlockSpec((IN_WO_ROWS, w_out.shape[1]), wo_slab),
        ],
        out_shape=[
            jax.ShapeDtypeStruct((Z_LEAD + t, D_MAIN), BF16),
            jax.ShapeDtypeStruct((t + IN_TM, RANK_PAD), F32),
            jax.ShapeDtypeStruct(w_out.shape, BF16),
        ],
        scratch_shapes=[pltpu.VMEM((IN_TM, D_MODEL), BF16), pltpu.VMEM((IN_TM, D_MODEL), BF16)],
        compiler_params=_params(2, IN_VMEM_LIMIT),
        name="inproj",
    )(x2, norm_w, mod, mod, w_a, w_main, w_out)


MIX_TR = 256


def _pool_constants():
    r = np.arange(MIX_TR)[:, None]
    c = np.arange(MIX_TR)[None, :]
    hr = np.arange(POOL_HALO)[:, None]
    hc = np.arange(POOL_HALO)[None, :]
    bands, hbands = [], []
    for w in POOL_WINDOWS:
        bands.append(np.where((c <= r) & (c > r - w), 1.0 / w, 0.0) - (r == c))
        hbands.append(np.where(hr + POOL_HALO - hc < w, 1.0 / w, 0.0))
    return (jnp.asarray(np.stack(bands), BF16), jnp.asarray(np.stack(hbands), BF16))


def _pool_tile(i, u_ref, up_ref, gp_ref, band_ref, hband_ref, y_ref):
    pad = jnp.zeros((MIX_TR - POOL_HALO, POOL_GROUP), F32)
    rows = lax.broadcasted_iota(jnp.int32, (POOL_HALO, 1), 0)
    for g, w in enumerate(POOL_WINDOWS):
        cs = slice(g * POOL_GROUP, (g + 1) * POOL_GROUP)
        u = u_ref[:, cs]
        pooled = _dot(band_ref[g], u) + jnp.concatenate([_dot(hband_ref[g], up_ref[:, cs]), pad], axis=0)
        ratio = jnp.where(i == 0, float(w) / jnp.minimum(rows + 1, w).astype(F32), 1.0)
        u_head = u[:POOL_HALO].astype(F32)
        head = ratio * (pooled[:POOL_HALO] + u_head) - u_head
        pooled = jnp.concatenate([head, pooled[POOL_HALO:]], axis=0)
        y_ref[:, cs] = pooled.astype(BF16) * _silu(gp_ref[:, cs])


GLA_TR = 256
GLA_LEVELS = GLA_TR.bit_length() - 1
GLA_MXU_LEVELS = 4


def _gla_constants():
    r = np.arange(GLA_TR)[:, None]
    t = np.arange(GLA_TR)[None, :]
    mats = []
    for k in range(GLA_MXU_LEVELS):
        half = ((r >> (k + 1)) << (k + 1)) + (1 << k)
        second = ((r >> k) & 1) == 1
        mats.append(np.where(second, (t >= half) & (t <= r), (t > r) & (t < half)))
    mats.append(t <= r)
    tcat = np.concatenate(mats, axis=0).astype(np.float32)
    diff = np.maximum(r ^ t, 1)
    level = np.where(t == r, GLA_LEVELS, np.where(t < r, np.floor(np.log2(diff)).astype(np.int64), -1))
    masks = np.stack([(level == k) for k in range(GLA_LEVELS + 1)]).astype(np.float32) * GLA_DK ** -0.5
    return jnp.asarray(tcat, BF16), jnp.asarray(masks, BF16)


def _gla_head(q_bf, k_bf, v, gg, g2, nw, tcat, mk_ref, st_ref, hd):
    tr, nl = GLA_TR, GLA_LEVELS
    g_hi, g_lo = _split_bf16(g2)
    n_small = GLA_MXU_LEVELS * tr
    dall = _dot(tcat[:n_small], g_hi)
    b2 = _dot(tcat[n_small:], g_hi) + _dot(tcat[n_small:], g_lo)
    rbit = lax.broadcasted_iota(jnp.int32, (tr, 1), 0)

    a_bf = mk_ref[nl] * _dot_nt(q_bf, k_bf).astype(BF16)
    for k in range(nl):
        if k < GLA_MXU_LEVELS:
            side = jnp.where(((rbit >> k) & 1) == 1, q_bf, k_bf)
            m = side * jnp.exp2(dall[k * tr:(k + 1) * tr]).astype(BF16)
        else:
            s = 1 << k
            parts = []
            for j in range(tr // (2 * s)):
                lo = 2 * s * j
                bp = b2[lo + s - 1:lo + s]
                parts.append(k_bf[lo:lo + s] * jnp.exp2(bp - b2[lo:lo + s]).astype(BF16))
                parts.append(q_bf[lo + s:lo + 2 * s] * jnp.exp2(b2[lo + s:lo + 2 * s] - bp).astype(BF16))
            m = jnp.concatenate(parts, axis=0)
        a_bf = a_bf + mk_ref[k] * _dot_nt(m, m).astype(BF16)

    b_end = b2[tr - 1:tr]
    q_in = q_bf * jnp.exp2(b2).astype(BF16)
    k_out = k_bf * jnp.exp2((b_end - 0.5 * math.log2(GLA_DK)) - b2).astype(BF16)
    st = st_ref[hd]
    o = _dot(a_bf, v) + _dot_nt(q_in, st.astype(BF16))
    st_ref[hd] = jnp.exp2(b_end) * st + _dot_tn(v, k_out)

    ms = jnp.mean(o * o, axis=-1, keepdims=True)
    on = (o * lax.rsqrt(ms + EPS)).astype(BF16)
    return on * (_silu(gg) * nw)


def _gate_rows(c_ref, wg_ref, gacc):
    p = c_ref[...] * wg_ref[...]
    gacc[...] += p.reshape(p.shape[0] // 8, 8, D_MODEL).sum(axis=0)


def _mixer_kernel(u_ref, up_ref, gp_ref, q_ref, k_ref, v_ref, gg_ref, a_ref,
                  band_ref, hband_ref, wa_ref, ba_ref, nw_ref, tc_ref, mk_ref,
                  c_ref, wg_ref, bg_ref, y_ref, gate_ref, st_ref, gacc):
    i = pl.program_id(0)

    @pl.when(i == 0)
    def _():
        st_ref[...] = jnp.zeros_like(st_ref)
        gacc[...] = jnp.zeros_like(gacc)

    _gate_rows(c_ref, wg_ref, gacc)
    _pool_tile(i, u_ref, up_ref, gp_ref, band_ref, hband_ref, y_ref)

    a = a_ref[...]
    a_hi = a.astype(BF16).astype(F32)
    lane = lax.broadcasted_iota(jnp.int32, a.shape, 1)
    a3 = jnp.where((lane >= GLA_RANK) & (lane < 2 * GLA_RANK), a - a_hi, a_hi).astype(BF16)
    w = wa_ref[...]
    w_hi = w.astype(BF16).astype(F32)
    wrow = lax.broadcasted_iota(jnp.int32, w.shape, 0)
    w3 = jnp.where(wrow >= 2 * GLA_RANK, w - w_hi, w_hi).astype(BF16)
    xl = (_dot(a3, w3) + ba_ref[...]) * LOG2E
    g2 = (jnp.minimum(xl, 0.0) - jnp.log2(1.0 + jnp.exp2(-jnp.abs(xl)))) * (1.0 / GLA_TAU)
    tcat = tc_ref[...]
    nw = nw_ref[...].astype(BF16)
    for hd in range(GLA_HEADS):
        ks = slice(hd * GLA_DK, (hd + 1) * GLA_DK)
        vs = slice(hd * GLA_DV, (hd + 1) * GLA_DV)
        ys = slice(D_POOL + hd * GLA_DV, D_POOL + (hd + 1) * GLA_DV)
        y_ref[:, ys] = _gla_head(q_ref[:, ks], k_ref[:, ks], v_ref[:, vs], gg_ref[:, vs],
                                 g2[:, ks], nw, tcat, mk_ref, st_ref, hd)

    @pl.when(i == pl.num_programs(0) - 1)
    def _():
        gate_ref[...] = gacc[...].sum(axis=0, keepdims=True) + bg_ref[...]


def _mixer(z, a_lr, w_alpha_pad, b_alpha, gla_norm_w, c_act, w_ada, b_ada):
    t = z.shape[0] - Z_LEAD
    n = t // MIX_TR
    gate_rows = D_MODEL // n
    assert MIX_TR == GLA_TR and gate_rows * n == D_MODEL and gate_rows % 8 == 0
    lead = Z_LEAD // MIX_TR
    hpt = MIX_TR // POOL_HALO
    tcat, masks = _gla_constants()
    bands, hbands = _pool_constants()
    const2 = lambda i: (0, 0)
    const3 = lambda i: (0, 0, 0)
    return pl.pallas_call(
        _mixer_kernel,
        grid=(n,),
        in_specs=[
            pl.BlockSpec((MIX_TR, D_POOL), lambda i: (lead + i, OFF_U // D_POOL)),
            pl.BlockSpec((POOL_HALO, D_POOL), lambda i: ((lead + i) * hpt - 1, OFF_U // D_POOL)),
            pl.BlockSpec((MIX_TR, D_POOL), lambda i: (lead + i, OFF_GP // D_POOL)),
            pl.BlockSpec((MIX_TR, GLA_KEY), lambda i: (lead + i, OFF_Q // GLA_KEY)),
            pl.BlockSpec((MIX_TR, GLA_KEY), lambda i: (lead + i, OFF_K // GLA_KEY)),
            pl.BlockSpec((MIX_TR, D_GLA), lambda i: (lead + i, OFF_V // D_GLA)),
            pl.BlockSpec((MIX_TR, D_GLA), lambda i: (lead + i, OFF_GG // D_GLA)),
            pl.BlockSpec((MIX_TR, RANK_PAD), lambda i: (i, 0)),
            pl.BlockSpec(bands.shape, const3),
            pl.BlockSpec(hbands.shape, const3),
            pl.BlockSpec((RANK_PAD, GLA_KEY), const2),
            pl.BlockSpec((1, GLA_KEY), const2),
            pl.BlockSpec((1, GLA_DV), const2),
            pl.BlockSpec(tcat.shape, const2),
            pl.BlockSpec(masks.shape, const3),
            pl.BlockSpec((gate_rows, 1), lambda i: (i, 0)),
            pl.BlockSpec((gate_rows, D_MODEL), lambda i: (i, 2)),
            pl.BlockSpec((1, D_MODEL), lambda i: (0, 2)),
        ],
        out_specs=[
            pl.BlockSpec((MIX_TR, D_POOL + D_GLA), lambda i: (i, 0)),
            pl.BlockSpec((1, D_MODEL), const2),
        ],
        out_shape=[
            jax.ShapeDtypeStruct((t, D_POOL + D_GLA), BF16),
            jax.ShapeDtypeStruct((1, D_MODEL), F32),
        ],
        scratch_shapes=[pltpu.VMEM((GLA_HEADS, GLA_DV, GLA_DK), F32), pltpu.VMEM((8, D_MODEL), F32)],
        compiler_params=_params(1),
        name="mixer",
    )(z, z, z, z, z, z, z, a_lr, bands, hbands, w_alpha_pad, b_alpha,
      gla_norm_w, tcat, masks, c_act, w_ada, b_ada)


OUT_TM = 256
OUT_NC = 512
OUT_VMEM_LIMIT = 63 * 1024 * 1024


def _outproj_kernel(y_ref, w_ref, x_ref, gate_ref, fw_ref, o_ref, xn_scr, inv_scr):
    i = pl.program_id(0)
    last = pl.num_programs(0) - 1
    chunks = [slice(c * OUT_NC, (c + 1) * OUT_NC) for c in range(D_MODEL // OUT_NC)]

    @pl.when(i == 0)
    def _():
        xn_scr[...] = jnp.zeros_like(xn_scr)
        inv_scr[...] = jnp.zeros_like(inv_scr)

    @pl.when(i < last)
    def _():
        inv_prev = inv_scr[...]
        ym = y_ref[...]
        ssq = jnp.zeros((OUT_TM, 1), F32)
        for cs in chunks:
            o_ref[:, cs] = xn_scr[:, cs] * inv_prev * fw_ref[:, cs]
            xn = x_ref[:, cs] + gate_ref[:, cs] * _dot(ym, w_ref[:, cs])
            xn_scr[:, cs] = xn
            ssq = ssq + jnp.sum(xn * xn, axis=-1, keepdims=True)
        inv_scr[...] = lax.rsqrt(ssq * (1.0 / D_MODEL) + EPS)

    @pl.when(i == last)
    def _():
        o_ref[...] = xn_scr[...] * inv_scr[...] * fw_ref[...]


def _outproj(y, w_out_bf, x2, gate, final_norm_w):
    t = x2.shape[0]
    n = t // OUT_TM
    cur = lambda i: (jnp.minimum(i, n - 1), 0)
    return pl.pallas_call(
        _outproj_kernel,
        grid=(n + 1,),
        in_specs=[
            pl.BlockSpec((OUT_TM, D_POOL + D_GLA), cur),
            pl.BlockSpec((D_MODEL, D_MODEL), lambda i: (0, 0), pipeline_mode=pl.Buffered(1)),
            pl.BlockSpec((OUT_TM, D_MODEL), cur),
            pl.BlockSpec((1, D_MODEL), lambda i: (0, 0)),
            pl.BlockSpec((1, D_MODEL), lambda i: (0, 0)),
        ],
        out_specs=pl.BlockSpec((OUT_TM, D_MODEL), lambda i: (jnp.maximum(i - 1, 0), 0)),
        out_shape=jax.ShapeDtypeStruct((t, D_MODEL), F32),
        scratch_shapes=[pltpu.VMEM((OUT_TM, D_MODEL), F32), pltpu.VMEM((OUT_TM, 1), F32)],
        compiler_params=_params(1, OUT_VMEM_LIMIT),
        name="outproj",
    )(y, w_out_bf, x2, gate, final_norm_w)


def kernel(x, c, w_ada, b_ada, norm_w, w_in, w_pool, pool_scale, w_alpha, b_alpha,
           gla_norm_w, w_out, final_norm_w):
    bsz, t, d = x.shape
    assert bsz == 1 and d == D_MODEL and w_ada.shape[0] == 1
    x2 = x.reshape(t, d)

    w_in_t = w_in[0].T
    mod, c_act, w_main = _prelude(c.reshape(d, 1), w_ada[0], b_ada, 2 * d, w_in_t,
                                  w_pool[0], pool_scale.reshape(D_POOL, 1))
    w_a = jnp.pad(jnp.tile(w_in_t[D_MAIN:], (3, 1)), ((0, RANK_PAD - 3 * GLA_RANK), (0, 0))).astype(BF16)
    z, a_lr, w_out_bf = _inproj(x2, norm_w, mod, w_a, w_main, w_out[0])

    w_alpha_pad = jnp.pad(jnp.tile(w_alpha[0], (3, 1)), ((0, RANK_PAD - 3 * GLA_RANK), (0, 0)))
    y, gate = _mixer(z, a_lr, w_alpha_pad, b_alpha, gla_norm_w,
                     c_act, w_ada[0], b_ada)

    out = _outproj(y, w_out_bf, x2, gate, final_norm_w.reshape(1, d))
    return out.reshape(bsz, t, d).astype(x.dtype)
```

```python
import math

import jax
import jax.numpy as jnp
import numpy as np
from jax import lax
from jax.experimental import pallas as pl
from jax.experimental.pallas import tpu as pltpu

F32 = jnp.float32
BF16 = jnp.bfloat16

D_MODEL = 4096
D_POOL = 2048
POOL_WINDOWS = (2, 4, 8, 16)
POOL_GROUP = 512
POOL_HALO = 16
D_GLA = 2048
GLA_HEADS = 4
GLA_DV = 512
GLA_DK = 256
GLA_KEY = 1024
GLA_RANK = 16
GLA_TAU = 16.0
EPS = 1e-6
D_MAIN = 2 * D_POOL + 2 * GLA_KEY + 2 * D_GLA
RANK_PAD = 128
LOG2E = math.log2(math.e)

OFF_U, OFF_GP, OFF_Q, OFF_K, OFF_V, OFF_GG = 0, 2048, 4096, 5120, 6144, 8192

VMEM_LIMIT = 56 * 1024 * 1024


def _dot(a, b):
    return jnp.dot(a, b, preferred_element_type=F32)


def _dot_nt(a, b):
    return lax.dot_general(a, b, (((1,), (1,)), ((), ())), preferred_element_type=F32)


def _dot_tn(a, b):
    return lax.dot_general(a, b, (((0,), (0,)), ((), ())), preferred_element_type=F32)


def _split_bf16(a):
    hi = a.astype(BF16)
    lo = (a - hi.astype(F32)).astype(BF16)
    return hi, lo


def _silu(v):
    return v * jax.nn.sigmoid(v)


def _params(n_axes, vmem_limit=VMEM_LIMIT):
    return pltpu.CompilerParams(
        dimension_semantics=("arbitrary",) * n_axes, vmem_limit_bytes=vmem_limit)


ADA_TN = 512
ADA_KC = 256
CAST_TM = 512


def _prelude_kernel(c_ref, w_ref, b_ref, wt_ref, wp_ref, ps_ref, o_ref, ca_ref, wbf_ref):
    tn = o_ref.shape[1]
    j = pl.program_id(0)

    @pl.when(j == 0)
    def _():
        ca_ref[...] = _silu(c_ref[...])

    @pl.when(j < D_POOL // CAST_TM)
    def _():
        folded = _dot_tn(wp_ref[0].astype(BF16), wt_ref[...].astype(BF16))
        wbf_ref[...] = (folded * ps_ref[...]).astype(wbf_ref.dtype)

    @pl.when(j >= D_POOL // CAST_TM)
    def _():
        wbf_ref[...] = wt_ref[...].astype(wbf_ref.dtype)

    def body(kk, acc):
        r = pl.multiple_of(kk * ADA_KC, ADA_KC)
        p = ca_ref[pl.ds(r, ADA_KC), :] * w_ref[pl.ds(r, ADA_KC), :]
        return acc + p.reshape(ADA_KC // 8, 8, tn).sum(axis=0)

    acc = lax.fori_loop(0, D_MODEL // ADA_KC, body, jnp.zeros((8, tn), F32))
    o_ref[...] = acc.sum(axis=0, keepdims=True) + b_ref[...]


def _prelude(c_col, w_ada, b_ada, n, w_t, w_pool, pool_scale_col):
    n_ada, n_cast = n // ADA_TN, D_MAIN // CAST_TM
    n_grp = D_POOL // CAST_TM
    assert n_cast >= n_ada and CAST_TM == POOL_GROUP and w_pool.shape[0] == n_grp
    ada_blk = lambda j: (0, jnp.minimum(j, n_ada - 1))
    return pl.pallas_call(
        _prelude_kernel,
        grid=(n_cast,),
        in_specs=[
            pl.BlockSpec((D_MODEL, 1), lambda j: (0, 0)),
            pl.BlockSpec((D_MODEL, ADA_TN), ada_blk),
            pl.BlockSpec((1, ADA_TN), ada_blk),
            pl.BlockSpec((CAST_TM, w_t.shape[1]), lambda j: (j, 0)),
            pl.BlockSpec((1, POOL_GROUP, POOL_GROUP), lambda j: (jnp.minimum(j, n_grp - 1), 0, 0)),
            pl.BlockSpec((POOL_GROUP, 1), lambda j: (jnp.minimum(j, n_grp - 1), 0)),
        ],
        out_specs=[
            pl.BlockSpec((1, ADA_TN), ada_blk),
            pl.BlockSpec((D_MODEL, 1), lambda j: (0, 0)),
            pl.BlockSpec((CAST_TM, w_t.shape[1]), lambda j: (j, 0)),
        ],
        out_shape=[
            jax.ShapeDtypeStruct((1, n), F32),
            jax.ShapeDtypeStruct((D_MODEL, 1), F32),
            jax.ShapeDtypeStruct((D_MAIN, w_t.shape[1]), BF16),
        ],
        compiler_params=_params(1),
        name="prelude",
    )(c_col, w_ada, b_ada, w_t, w_pool, pool_scale_col)


IN_TM = 1024
IN_TN = 1280
IN_SLAB = IN_TM // (D_MAIN // IN_TN)
IN_NSPLIT = 768
IN_WO_ROWS = 64
IN_VMEM_LIMIT = 60 * 1024 * 1024
Z_LEAD = IN_TM


def _inproj_step(h_src, h_dst, j, x_ref, nw_ref, shift_ref, scale_ref, wa_ref, w_ref, wo_ref,
                 z_ref, a_ref, wo_bf_ref):
    xs = x_ref[...]
    ms = jnp.mean(xs * xs, axis=-1, keepdims=True)
    mul = nw_ref[...] * (1.0 + scale_ref[...])
    h = (xs * lax.rsqrt(ms + EPS) * mul + shift_ref[...]).astype(BF16)
    h_dst[pl.ds(pl.multiple_of(j * IN_SLAB, IN_SLAB), IN_SLAB), :] = h
    if h_src is None:
        z_ref[...] = jnp.zeros_like(z_ref)
        a_ref[...] = _dot_nt(h, wa_ref[...])
    else:
        hs = h_src[...]
        z_ref[:, :IN_NSPLIT] = _dot_nt(hs, w_ref[:IN_NSPLIT, :]).astype(z_ref.dtype)
        a_ref[...] = _dot_nt(h, wa_ref[...])
        z_ref[:, IN_NSPLIT:] = _dot_nt(hs, w_ref[IN_NSPLIT:, :]).astype(z_ref.dtype)
    wo_bf_ref[...] = wo_ref[...].astype(wo_bf_ref.dtype)


def _inproj_kernel(x_ref, nw_ref, shift_ref, scale_ref, wa_ref, w_ref, wo_ref,
                   z_ref, a_ref, wo_bf_ref, h_even, h_odd):
    i = pl.program_id(0)
    j = pl.program_id(1)
    refs = (x_ref, nw_ref, shift_ref, scale_ref, wa_ref, w_ref, wo_ref, z_ref, a_ref, wo_bf_ref)
    pl.when(i == 0)(lambda: _inproj_step(None, h_even, j, *refs))
    pl.when((i & 1) == 1)(lambda: _inproj_step(h_even, h_odd, j, *refs))
    pl.when(jnp.logical_and(i > 0, (i & 1) == 0))(lambda: _inproj_step(h_odd, h_even, j, *refs))


def _inproj(x2, norm_w, mod, w_a, w_main, w_out):
    t = x2.shape[0]
    n_i, n_j = t // IN_TM, D_MAIN // IN_TN
    n_wo = w_out.shape[0] // IN_WO_ROWS
    assert IN_SLAB * n_j == IN_TM and (n_i + 1) * n_j >= n_wo

    def x_slab(i, j):
        return (jnp.minimum(i, n_i - 1) * n_j + j, 0)

    def wo_slab(i, j):
        return (jnp.minimum(i * n_j + j, n_wo - 1), 0)

    return pl.pallas_call(
        _inproj_kernel,
        grid=(n_i + 1, n_j),
        in_specs=[
            pl.BlockSpec((IN_SLAB, D_MODEL), x_slab),
            pl.BlockSpec((1, D_MODEL), lambda i, j: (0, 0)),
            pl.BlockSpec((1, D_MODEL), lambda i, j: (0, 0)),
            pl.BlockSpec((1, D_MODEL), lambda i, j: (0, 1)),
            pl.BlockSpec((RANK_PAD, D_MODEL), lambda i, j: (0, 0)),
            pl.BlockSpec((IN_TN, D_MODEL), lambda i, j: (jnp.where(i == 0, 0, j), 0)),
            pl.BlockSpec((IN_WO_ROWS, w_out.shape[1]), wo_slab),
        ],
        out_specs=[
            pl.BlockSpec((IN_TM, IN_TN), lambda i, j: (i, j)),
            pl.BlockSpec((IN_SLAB, RANK_PAD), lambda i, j: (i * n_j + j, 0)),
            pl.BlockSpec((IN_WO_ROWS, w_out.shape[1]), wo_slab),
        ],
        out_shape=[
            jax.ShapeDtypeStruct((Z_LEAD + t, D_MAIN), BF16),
            jax.ShapeDtypeStruct((t + IN_TM, RANK_PAD), F32),
            jax.ShapeDtypeStruct(w_out.shape, BF16),
        ],
        scratch_shapes=[pltpu.VMEM((IN_TM, D_MODEL), BF16), pltpu.VMEM((IN_TM, D_MODEL), BF16)],
        compiler_params=_params(2, IN_VMEM_LIMIT),
        name="inproj",
    )(x2, norm_w, mod, mod, w_a, w_main, w_out)


MIX_TR = 256


def _pool_constants():
    r = np.arange(MIX_TR)[:, None]
    c = np.arange(MIX_TR)[None, :]
    hr = np.arange(POOL_HALO)[:, None]
    hc = np.arange(POOL_HALO)[None, :]
    bands, hbands = [], []
    for w in POOL_WINDOWS:
        bands.append(np.where((c <= r) & (c > r - w), 1.0 / w, 0.0) - (r == c))
        hbands.append(np.where(hr + POOL_HALO - hc < w, 1.0 / w, 0.0))
    return (jnp.asarray(np.stack(bands), BF16), jnp.asarray(np.stack(hbands), BF16))


def _pool_tile(i, u_ref, up_ref, gp_ref, band_ref, hband_ref, y_ref):
    pad = jnp.zeros((MIX_TR - POOL_HALO, POOL_GROUP), F32)
    rows = lax.broadcasted_iota(jnp.int32, (POOL_HALO, 1), 0)
    for g, w in enumerate(POOL_WINDOWS):
        cs = slice(g * POOL_GROUP, (g + 1) * POOL_GROUP)
        u = u_ref[:, cs]
        pooled = _dot(band_ref[g], u) + jnp.concatenate([_dot(hband_ref[g], up_ref[:, cs]), pad], axis=0)
        ratio = jnp.where(i == 0, float(w) / jnp.minimum(rows + 1, w).astype(F32), 1.0)
        u_head = u[:POOL_HALO].astype(F32)
        head = ratio * (pooled[:POOL_HALO] + u_head) - u_head
        pooled = jnp.concatenate([head, pooled[POOL_HALO:]], axis=0)
        y_ref[:, cs] = pooled.astype(BF16) * _silu(gp_ref[:, cs])


GLA_TR = 256
GLA_LEVELS = GLA_TR.bit_length() - 1
GLA_MXU_LEVELS = 4


def _gla_constants():
    r = np.arange(GLA_TR)[:, None]
    t = np.arange(GLA_TR)[None, :]
    mats = []
    for k in range(1, GLA_MXU_LEVELS):
        half = ((r >> (k + 1)) << (k + 1)) + (1 << k)
        second = ((r >> k) & 1) == 1
        mats.append(np.where(second, (t >= half) & (t <= r), (t > r) & (t < half)))
    mats.append(t <= r)
    tcat = np.concatenate(mats, axis=0).astype(np.float32)
    diff = np.maximum(r ^ t, 1)
    level = np.where(t == r, GLA_LEVELS, np.where(t < r, np.floor(np.log2(diff)).astype(np.int64), -1))
    masks = np.stack([(level == k) for k in range(GLA_LEVELS + 1)]).astype(np.float32) * GLA_DK ** -0.5
    return jnp.asarray(tcat, BF16), jnp.asarray(masks, BF16)


def _gla_head(q_bf, k_bf, v, gg, g2, nw, tcat, mk_ref, st_ref, hd):
    tr, nl = GLA_TR, GLA_LEVELS
    g_hi, g_lo = _split_bf16(g2)
    n_small = (GLA_MXU_LEVELS - 1) * tr
    dall = _dot(tcat[:n_small], g_hi)
    b2 = _dot(tcat[n_small:], g_hi) + _dot(tcat[n_small:], g_lo)
    rbit = lax.broadcasted_iota(jnp.int32, (tr, 1), 0)

    a_bf = mk_ref[nl] * _dot_nt(q_bf, k_bf).astype(BF16)
    for k in range(nl):
        if k == 0:
            m = jnp.where((rbit & 1) == 1, q_bf * jnp.exp2(g2).astype(BF16), k_bf)
        elif k < GLA_MXU_LEVELS:
            side = jnp.where(((rbit >> k) & 1) == 1, q_bf, k_bf)
            m = side * jnp.exp2(dall[(k - 1) * tr:k * tr]).astype(BF16)
        else:
            s = 1 << k
            parts = []
            for j in range(tr // (2 * s)):
                lo = 2 * s * j
                bp = b2[lo + s - 1:lo + s]
                parts.append(k_bf[lo:lo + s] * jnp.exp2(bp - b2[lo:lo + s]).astype(BF16))
                parts.append(q_bf[lo + s:lo + 2 * s] * jnp.exp2(b2[lo + s:lo + 2 * s] - bp).astype(BF16))
            m = jnp.concatenate(parts, axis=0)
        a_bf = a_bf + mk_ref[k] * _dot_nt(m, m).astype(BF16)

    b_end = b2[tr - 1:tr]
    q_in = q_bf * jnp.exp2(b2).astype(BF16)
    k_out = k_bf * jnp.exp2((b_end - 0.5 * math.log2(GLA_DK)) - b2).astype(BF16)
    st = st_ref[hd]
    o = _dot(a_bf, v) + _dot_nt(q_in, st.astype(BF16))
    st_ref[hd] = jnp.exp2(b_end) * st + _dot_tn(v, k_out)

    ms = jnp.mean(o * o, axis=-1, keepdims=True)
    on = (o * lax.rsqrt(ms + EPS)).astype(BF16)
    return on * (_silu(gg) * nw)


MIX_GATE_TN = 128
MIX_GATE_KC = 256


def _gate_block(c_ref, wg_ref, bg_ref, gate_ref):
    acc = jnp.zeros((8, MIX_GATE_TN), F32)
    for kk in range(D_MODEL // MIX_GATE_KC):
        rs = slice(kk * MIX_GATE_KC, (kk + 1) * MIX_GATE_KC)
        p = c_ref[rs, :] * wg_ref[rs, :]
        acc = acc + p.reshape(MIX_GATE_KC // 8, 8, MIX_GATE_TN).sum(axis=0)
    gate_ref[...] = acc.sum(axis=0, keepdims=True) + bg_ref[...]


def _mixer_kernel(u_ref, up_ref, gp_ref, q_ref, k_ref, v_ref, gg_ref, a_ref,
                  band_ref, hband_ref, wa_ref, ba_ref, nw_ref, tc_ref, mk_ref,
                  c_ref, wg_ref, bg_ref, y_ref, gate_ref, st_ref):
    i = pl.program_id(0)

    @pl.when(i == 0)
    def _():
        st_ref[...] = jnp.zeros_like(st_ref)

    _gate_block(c_ref, wg_ref, bg_ref, gate_ref)
    _pool_tile(i, u_ref, up_ref, gp_ref, band_ref, hband_ref, y_ref)

    a = a_ref[...]
    a_hi = a.astype(BF16).astype(F32)
    lane = lax.broadcasted_iota(jnp.int32, a.shape, 1)
    a3 = jnp.where((lane >= GLA_RANK) & (lane < 2 * GLA_RANK), a - a_hi, a_hi).astype(BF16)
    w = wa_ref[...]
    w_hi = w.astype(BF16).astype(F32)
    wrow = lax.broadcasted_iota(jnp.int32, w.shape, 0)
    w3 = jnp.where(wrow >= 2 * GLA_RANK, w - w_hi, w_hi).astype(BF16)
    xl = (_dot(a3, w3) + ba_ref[...]) * LOG2E
    g2 = (jnp.minimum(xl, 0.0) - jnp.log2(1.0 + jnp.exp2(-jnp.abs(xl)))) * (1.0 / GLA_TAU)
    tcat = tc_ref[...]
    nw = nw_ref[...].astype(BF16)
    for hd in range(GLA_HEADS):
        ks = slice(hd * GLA_DK, (hd + 1) * GLA_DK)
        vs = slice(hd * GLA_DV, (hd + 1) * GLA_DV)
        ys = slice(D_POOL + hd * GLA_DV, D_POOL + (hd + 1) * GLA_DV)
        y_ref[:, ys] = _gla_head(q_ref[:, ks], k_ref[:, ks], v_ref[:, vs], gg_ref[:, vs],
                                 g2[:, ks], nw, tcat, mk_ref, st_ref, hd)


def _mixer(z, a_lr, w_alpha_pad, b_alpha, gla_norm_w, c_act, w_ada, b_ada):
    t = z.shape[0] - Z_LEAD
    n = t // MIX_TR
    assert MIX_TR == GLA_TR and n * MIX_GATE_TN == D_MODEL
    lead = Z_LEAD // MIX_TR
    hpt = MIX_TR // POOL_HALO
    gate0 = 2 * D_MODEL // MIX_GATE_TN
    tcat, masks = _gla_constants()
    bands, hbands = _pool_constants()
    const2 = lambda i: (0, 0)
    const3 = lambda i: (0, 0, 0)
    return pl.pallas_call(
        _mixer_kernel,
        grid=(n,),
        in_specs=[
            pl.BlockSpec((MIX_TR, D_POOL), lambda i: (lead + i, OFF_U // D_POOL)),
            pl.BlockSpec((POOL_HALO, D_POOL), lambda i: ((lead + i) * hpt - 1, OFF_U // D_POOL)),
            pl.BlockSpec((MIX_TR, D_POOL), lambda i: (lead + i, OFF_GP // D_POOL)),
            pl.BlockSpec((MIX_TR, GLA_KEY), lambda i: (lead + i, OFF_Q // GLA_KEY)),
            pl.BlockSpec((MIX_TR, GLA_KEY), lambda i: (lead + i, OFF_K // GLA_KEY)),
            pl.BlockSpec((MIX_TR, D_GLA), lambda i: (lead + i, OFF_V // D_GLA)),
            pl.BlockSpec((MIX_TR, D_GLA), lambda i: (lead + i, OFF_GG // D_GLA)),
            pl.BlockSpec((MIX_TR, RANK_PAD), lambda i: (i, 0)),
            pl.BlockSpec(bands.shape, const3),
            pl.BlockSpec(hbands.shape, const3),
            pl.BlockSpec((RANK_PAD, GLA_KEY), const2),
            pl.BlockSpec((1, GLA_KEY), const2),
            pl.BlockSpec((1, GLA_DV), const2),
            pl.BlockSpec(tcat.shape, const2),
            pl.BlockSpec(masks.shape, const3),
            pl.BlockSpec((D_MODEL, 1), const2),
            pl.BlockSpec((D_MODEL, MIX_GATE_TN), lambda i: (0, gate0 + i)),
            pl.BlockSpec((1, MIX_GATE_TN), lambda i: (0, gate0 + i)),
        ],
        out_specs=[
            pl.BlockSpec((MIX_TR, D_POOL + D_GLA), lambda i: (i, 0)),
            pl.BlockSpec((1, MIX_GATE_TN), lambda i: (0, i)),
        ],
        out_shape=[
            jax.ShapeDtypeStruct((t, D_POOL + D_GLA), BF16),
            jax.ShapeDtypeStruct((1, D_MODEL), F32),
        ],
        scratch_shapes=[pltpu.VMEM((GLA_HEADS, GLA_DV, GLA_DK), F32)],
        compiler_params=_params(1),
        name="mixer",
    )(z, z, z, z, z, z, z, a_lr, bands, hbands, w_alpha_pad, b_alpha,
      gla_norm_w, tcat, masks, c_act, w_ada, b_ada)


OUT_TM = 256
OUT_NC = 512
OUT_VMEM_LIMIT = 63 * 1024 * 1024


def _outproj_kernel(y_ref, w_ref, x_ref, gate_ref, fw_ref, o_ref, xn_scr, inv_scr):
    i = pl.program_id(0)
    last = pl.num_programs(0) - 1
    chunks = [slice(c * OUT_NC, (c + 1) * OUT_NC) for c in range(D_MODEL // OUT_NC)]

    @pl.when(i == 0)
    def _():
        xn_scr[...] = jnp.zeros_like(xn_scr)
        inv_scr[...] = jnp.zeros_like(inv_scr)

    @pl.when(i < last)
    def _():
        inv_prev = inv_scr[...]
        ym = y_ref[...]
        ssq = jnp.zeros((OUT_TM, 1), F32)
        for cs in chunks:
            o_ref[:, cs] = xn_scr[:, cs] * inv_prev * fw_ref[:, cs]
            xn = x_ref[:, cs] + gate_ref[:, cs] * _dot(ym, w_ref[:, cs])
            xn_scr[:, cs] = xn
            ssq = ssq + jnp.sum(xn * xn, axis=-1, keepdims=True)
        inv_scr[...] = lax.rsqrt(ssq * (1.0 / D_MODEL) + EPS)

    @pl.when(i == last)
    def _():
        o_ref[...] = xn_scr[...] * inv_scr[...] * fw_ref[...]


def _outproj(y, w_out_bf, x2, gate, final_norm_w):
    t = x2.shape[0]
    n = t // OUT_TM
    cur = lambda i: (jnp.minimum(i, n - 1), 0)
    return pl.pallas_call(
        _outproj_kernel,
        grid=(n + 1,),
        in_specs=[
            pl.BlockSpec((OUT_TM, D_POOL + D_GLA), cur),
            pl.BlockSpec((D_MODEL, D_MODEL), lambda i: (0, 0), pipeline_mode=pl.Buffered(1)),
            pl.BlockSpec((OUT_TM, D_MODEL), cur),
            pl.BlockSpec((1, D_MODEL), lambda i: (0, 0)),
            pl.BlockSpec((1, D_MODEL), lambda i: (0, 0)),
        ],
        out_specs=pl.BlockSpec((OUT_TM, D_MODEL), lambda i: (jnp.maximum(i - 1, 0), 0)),
        out_shape=jax.ShapeDtypeStruct((t, D_MODEL), F32),
        scratch_shapes=[pltpu.VMEM((OUT_TM, D_MODEL), F32), pltpu.VMEM((OUT_TM, 1), F32)],
        compiler_params=_params(1, OUT_VMEM_LIMIT),
        name="outproj",
    )(y, w_out_bf, x2, gate, final_norm_w)


def kernel(x, c, w_ada, b_ada, norm_w, w_in, w_pool, pool_scale, w_alpha, b_alpha,
           gla_norm_w, w_out, final_norm_w):
    bsz, t, d = x.shape
    assert bsz == 1 and d == D_MODEL and w_ada.shape[0] == 1
    x2 = x.reshape(t, d)

    w_in_t = w_in[0].T
    mod, c_act, w_main = _prelude(c.reshape(d, 1), w_ada[0], b_ada, 2 * d, w_in_t,
                                  w_pool[0], pool_scale.reshape(D_POOL, 1))
    w_a = jnp.pad(jnp.tile(w_in_t[D_MAIN:], (3, 1)), ((0, RANK_PAD - 3 * GLA_RANK), (0, 0))).astype(BF16)
    z, a_lr, w_out_bf = _inproj(x2, norm_w, mod, w_a, w_main, w_out[0])

    w_alpha_pad = jnp.pad(jnp.tile(w_alpha[0], (3, 1)), ((0, RANK_PAD - 3 * GLA_RANK), (0, 0)))
    y, gate = _mixer(z, a_lr, w_alpha_pad, b_alpha, gla_norm_w,
                     c_act, w_ada[0], b_ada)

    out = _outproj(y, w_out_bf, x2, gate, final_norm_w.reshape(1, d))
    return out.reshape(bsz, t, d).astype(x.dtype)
```

```python
import math

import jax
import jax.numpy as jnp
import numpy as np
from jax import lax
from jax.experimental import pallas as pl
from jax.experimental.pallas import tpu as pltpu

F32 = jnp.float32
BF16 = jnp.bfloat16

D_MODEL = 4096
D_POOL = 2048
POOL_WINDOWS = (2, 4, 8, 16)
POOL_GROUP = 512
POOL_HALO = 16
D_GLA = 2048
GLA_HEADS = 4
GLA_DV = 512
GLA_DK = 256
GLA_KEY = 1024
GLA_RANK = 16
GLA_TAU = 16.0
EPS = 1e-6
D_MAIN = 2 * D_POOL + 2 * GLA_KEY + 2 * D_GLA
RANK_PAD = 128
LOG2E = math.log2(math.e)

OFF_U, OFF_GP, OFF_Q, OFF_K, OFF_V, OFF_GG = 0, 2048, 4096, 5120, 6144, 8192

VMEM_LIMIT = 56 * 1024 * 1024


def _dot(a, b):
    return jnp.dot(a, b, preferred_element_type=F32)


def _dot_nt(a, b):
    return lax.dot_general(a, b, (((1,), (1,)), ((), ())), preferred_element_type=F32)


def _dot_tn(a, b):
    return lax.dot_general(a, b, (((0,), (0,)), ((), ())), preferred_element_type=F32)


def _split_bf16(a):
    hi = a.astype(BF16)
    lo = (a - hi.astype(F32)).astype(BF16)
    return hi, lo


def _silu(v):
    return v * jax.nn.sigmoid(v)


def _params(n_axes, vmem_limit=VMEM_LIMIT):
    return pltpu.CompilerParams(
        dimension_semantics=("arbitrary",) * n_axes, vmem_limit_bytes=vmem_limit)


ADA_TN = 512
ADA_KC = 256
CAST_TM = 512


def _prelude_kernel(c_ref, w_ref, b_ref, wt_ref, wp_ref, ps_ref, o_ref, ca_ref, wbf_ref):
    tn = o_ref.shape[1]
    j = pl.program_id(0)

    @pl.when(j == 0)
    def _():
        ca_ref[...] = _silu(c_ref[...])

    @pl.when(j < D_POOL // CAST_TM)
    def _():
        wp = (wp_ref[0] * ps_ref[...]).astype(BF16)
        wbf_ref[...] = _dot_tn(wp, wt_ref[...].astype(BF16)).astype(wbf_ref.dtype)

    @pl.when(j >= D_POOL // CAST_TM)
    def _():
        wbf_ref[...] = wt_ref[...].astype(wbf_ref.dtype)

    def body(kk, acc):
        r = pl.multiple_of(kk * ADA_KC, ADA_KC)
        p = ca_ref[pl.ds(r, ADA_KC), :] * w_ref[pl.ds(r, ADA_KC), :]
        return acc + p.reshape(ADA_KC // 8, 8, tn).sum(axis=0)

    acc = lax.fori_loop(0, D_MODEL // ADA_KC, body, jnp.zeros((8, tn), F32))
    o_ref[...] = acc.sum(axis=0, keepdims=True) + b_ref[...]


def _prelude(c_col, w_ada, b_ada, n, w_t, w_pool, pool_scale):
    n_ada, n_cast = n // ADA_TN, D_MAIN // CAST_TM
    n_grp = D_POOL // CAST_TM
    assert n_cast >= n_ada and CAST_TM == POOL_GROUP and w_pool.shape[0] == n_grp
    ada_blk = lambda j: (0, jnp.minimum(j, n_ada - 1))
    return pl.pallas_call(
        _prelude_kernel,
        grid=(n_cast,),
        in_specs=[
            pl.BlockSpec((D_MODEL, 1), lambda j: (0, 0)),
            pl.BlockSpec((D_MODEL, ADA_TN), ada_blk),
            pl.BlockSpec((1, ADA_TN), ada_blk),
            pl.BlockSpec((CAST_TM, w_t.shape[1]), lambda j: (j, 0)),
            pl.BlockSpec((1, POOL_GROUP, POOL_GROUP), lambda j: (jnp.minimum(j, n_grp - 1), 0, 0)),
            pl.BlockSpec((1, POOL_GROUP), lambda j: (0, jnp.minimum(j, n_grp - 1))),
        ],
        out_specs=[
            pl.BlockSpec((1, ADA_TN), ada_blk),
            pl.BlockSpec((D_MODEL, 1), lambda j: (0, 0)),
            pl.BlockSpec((CAST_TM, w_t.shape[1]), lambda j: (j, 0)),
        ],
        out_shape=[
            jax.ShapeDtypeStruct((1, n), F32),
            jax.ShapeDtypeStruct((D_MODEL, 1), F32),
            jax.ShapeDtypeStruct((D_MAIN, w_t.shape[1]), BF16),
        ],
        compiler_params=_params(1),
        name="prelude",
    )(c_col, w_ada, b_ada, w_t, w_pool, pool_scale)


IN_TM = 1024
IN_TN = 1280
IN_SLAB = IN_TM // (D_MAIN // IN_TN)
IN_NSPLIT = 768
IN_WO_ROWS = 64
IN_VMEM_LIMIT = 60 * 1024 * 1024
Z_LEAD = IN_TM


def _inproj_step(h_src, h_dst, j, x_ref, nw_ref, shift_ref, scale_ref, wa_ref, w_ref, wo_ref,
                 z_ref, a_ref, wo_bf_ref):
    xs = x_ref[...]
    ms = jnp.mean(xs * xs, axis=-1, keepdims=True)
    mul = nw_ref[...] * (1.0 + scale_ref[...])
    h = (xs * lax.rsqrt(ms + EPS) * mul + shift_ref[...]).astype(BF16)
    h_dst[pl.ds(pl.multiple_of(j * IN_SLAB, IN_SLAB), IN_SLAB), :] = h
    if h_src is None:
        z_ref[...] = jnp.zeros_like(z_ref)
        a_ref[...] = _dot_nt(h, wa_ref[...])
    else:
        hs = h_src[...]
        z_ref[:, :IN_NSPLIT] = _dot_nt(hs, w_ref[:IN_NSPLIT, :]).astype(z_ref.dtype)
        a_ref[...] = _dot_nt(h, wa_ref[...])
        z_ref[:, IN_NSPLIT:] = _dot_nt(hs, w_ref[IN_NSPLIT:, :]).astype(z_ref.dtype)
    wo_bf_ref[...] = wo_ref[...].astype(wo_bf_ref.dtype)


def _inproj_kernel(x_ref, nw_ref, shift_ref, scale_ref, wa_ref, w_ref, wo_ref,
                   z_ref, a_ref, wo_bf_ref, h_even, h_odd):
    i = pl.program_id(0)
    j = pl.program_id(1)
    refs = (x_ref, nw_ref, shift_ref, scale_ref, wa_ref, w_ref, wo_ref, z_ref, a_ref, wo_bf_ref)
    pl.when(i == 0)(lambda: _inproj_step(None, h_even, j, *refs))
    pl.when((i & 1) == 1)(lambda: _inproj_step(h_even, h_odd, j, *refs))
    pl.when(jnp.logical_and(i > 0, (i & 1) == 0))(lambda: _inproj_step(h_odd, h_even, j, *refs))


def _inproj(x2, norm_w, mod, w_a, w_main, w_out):
    t = x2.shape[0]
    n_i, n_j = t // IN_TM, D_MAIN // IN_TN
    n_wo = w_out.shape[0] // IN_WO_ROWS
    assert IN_SLAB * n_j == IN_TM and (n_i + 1) * n_j >= n_wo

    def x_slab(i, j):
        return (jnp.minimum(i, n_i - 1) * n_j + j, 0)

    def wo_slab(i, j):
        return (jnp.minimum(i * n_j + j, n_wo - 1), 0)

    return pl.pallas_call(
        _inproj_kernel,
        grid=(n_i + 1, n_j),
        in_specs=[
            pl.BlockSpec((IN_SLAB, D_MODEL), x_slab),
            pl.BlockSpec((1, D_MODEL), lambda i, j: (0, 0)),
            pl.BlockSpec((1, D_MODEL), lambda i, j: (0, 0)),
            pl.BlockSpec((1, D_MODEL), lambda i, j: (0, 1)),
            pl.BlockSpec((RANK_PAD, D_MODEL), lambda i, j: (0, 0)),
            pl.BlockSpec((IN_TN, D_MODEL), lambda i, j: (jnp.where(i == 0, 0, j), 0)),
            pl.BlockSpec((IN_WO_ROWS, w_out.shape[1]), wo_slab),
        ],
        out_specs=[
            pl.BlockSpec((IN_TM, IN_TN), lambda i, j: (i, j)),
            pl.BlockSpec((IN_SLAB, RANK_PAD), lambda i, j: (i * n_j + j, 0)),
            pl.BlockSpec((IN_WO_ROWS, w_out.shape[1]), wo_slab),
        ],
        out_shape=[
            jax.ShapeDtypeStruct((Z_LEAD + t, D_MAIN), BF16),
            jax.ShapeDtypeStruct((t + IN_TM, RANK_PAD), F32),
            jax.ShapeDtypeStruct(w_out.shape, BF16),
        ],
        scratch_shapes=[pltpu.VMEM((IN_TM, D_MODEL), BF16), pltpu.VMEM((IN_TM, D_MODEL), BF16)],
        compiler_params=_params(2, IN_VMEM_LIMIT),
        name="inproj",
    )(x2, norm_w, mod, mod, w_a, w_main, w_out)


MIX_TR = 256


def _pool_constants():
    r = np.arange(MIX_TR)[:, None]
    c = np.arange(MIX_TR)[None, :]
    hr = np.arange(POOL_HALO)[:, None]
    hc = np.arange(POOL_HALO)[None, :]
    bands, hbands = [], []
    for w in POOL_WINDOWS:
        bands.append(np.where((c <= r) & (c > r - w), 1.0 / w, 0.0) - (r == c))
        hbands.append(np.where(hr + POOL_HALO - hc < w, 1.0 / w, 0.0))
    return (jnp.asarray(np.stack(bands), BF16), jnp.asarray(np.stack(hbands), BF16))


def _pool_tile(i, u_ref, up_ref, gp_ref, band_ref, hband_ref, y_ref):
    pad = jnp.zeros((MIX_TR - POOL_HALO, POOL_GROUP), F32)
    rows = lax.broadcasted_iota(jnp.int32, (POOL_HALO, 1), 0)
    for g, w in enumerate(POOL_WINDOWS):
        cs = slice(g * POOL_GROUP, (g + 1) * POOL_GROUP)
        u = u_ref[:, cs]
        pooled = _dot(band_ref[g], u) + jnp.concatenate([_dot(hband_ref[g], up_ref[:, cs]), pad], axis=0)
        ratio = jnp.where(i == 0, float(w) / jnp.minimum(rows + 1, w).astype(F32), 1.0)
        u_head = u[:POOL_HALO].astype(F32)
        head = ratio * (pooled[:POOL_HALO] + u_head) - u_head
        pooled = jnp.concatenate([head, pooled[POOL_HALO:]], axis=0)
        y_ref[:, cs] = pooled.astype(BF16) * _silu(gp_ref[:, cs])


GLA_TR = 256
GLA_LEVELS = GLA_TR.bit_length() - 1
GLA_MXU_LEVELS = 4


def _gla_constants():
    r = np.arange(GLA_TR)[:, None]
    t = np.arange(GLA_TR)[None, :]
    mats = []
    for k in range(1, GLA_MXU_LEVELS):
        half = ((r >> (k + 1)) << (k + 1)) + (1 << k)
        second = ((r >> k) & 1) == 1
        mats.append(np.where(second, (t >= half) & (t <= r), (t > r) & (t < half)))
    mats.append(t <= r)
    tcat = np.concatenate(mats, axis=0).astype(np.float32)
    diff = np.maximum(r ^ t, 1)
    level = np.where(t == r, GLA_LEVELS, np.where(t < r, np.floor(np.log2(diff)).astype(np.int64), -1))
    masks = np.stack([(level == k) for k in range(GLA_LEVELS + 1)]).astype(np.float32) * GLA_DK ** -0.5
    return jnp.asarray(tcat, BF16), jnp.asarray(masks, BF16)


def _gla_head(q_bf, k_bf, v, gg, g2, nw, tcat, mk_ref, st_ref, hd):
    tr, nl = GLA_TR, GLA_LEVELS
    g_hi, g_lo = _split_bf16(g2)
    n_small = (GLA_MXU_LEVELS - 1) * tr
    dall = _dot(tcat[:n_small], g_hi)
    b2 = _dot(tcat[n_small:], g_hi) + _dot(tcat[n_small:], g_lo)
    rbit = lax.broadcasted_iota(jnp.int32, (tr, 1), 0)

    a_bf = mk_ref[nl] * _dot_nt(q_bf, k_bf).astype(BF16)
    for k in range(nl):
        if k == 0:
            m = jnp.where((rbit & 1) == 1, q_bf * jnp.exp2(g2).astype(BF16), k_bf)
        elif k < GLA_MXU_LEVELS:
            side = jnp.where(((rbit >> k) & 1) == 1, q_bf, k_bf)
            m = side * jnp.exp2(dall[(k - 1) * tr:k * tr]).astype(BF16)
        else:
            s = 1 << k
            parts = []
            for j in range(tr // (2 * s)):
                lo = 2 * s * j
                bp = b2[lo + s - 1:lo + s]
                parts.append(k_bf[lo:lo + s] * jnp.exp2(bp - b2[lo:lo + s]).astype(BF16))
                parts.append(q_bf[lo + s:lo + 2 * s] * jnp.exp2(b2[lo + s:lo + 2 * s] - bp).astype(BF16))
            m = jnp.concatenate(parts, axis=0)
        a_bf = a_bf + mk_ref[k] * _dot_nt(m, m).astype(BF16)

    b_end = b2[tr - 1:tr]
    q_in = q_bf * jnp.exp2(b2).astype(BF16)
    k_out = k_bf * jnp.exp2((b_end - 0.5 * math.log2(GLA_DK)) - b2).astype(BF16)
    st = st_ref[hd]
    o = _dot(a_bf, v) + _dot_nt(q_in, st.astype(BF16))
    st_ref[hd] = jnp.exp2(b_end) * st + _dot_tn(v, k_out)

    ms = jnp.mean(o * o, axis=-1, keepdims=True)
    on = (o * lax.rsqrt(ms + EPS)).astype(BF16)
    return on * (_silu(gg) * nw)


MIX_GATE_TN = 128
MIX_GATE_KC = 256


def _gate_block(c_ref, wg_ref, bg_ref, gate_ref):
    acc = jnp.zeros((8, MIX_GATE_TN), F32)
    for kk in range(D_MODEL // MIX_GATE_KC):
        rs = slice(kk * MIX_GATE_KC, (kk + 1) * MIX_GATE_KC)
        p = c_ref[rs, :] * wg_ref[rs, :]
        acc = acc + p.reshape(MIX_GATE_KC // 8, 8, MIX_GATE_TN).sum(axis=0)
    gate_ref[...] = acc.sum(axis=0, keepdims=True) + bg_ref[...]


def _mixer_kernel(u_ref, up_ref, gp_ref, q_ref, k_ref, v_ref, gg_ref, a_ref,
                  band_ref, hband_ref, wa_ref, ba_ref, nw_ref, tc_ref, mk_ref,
                  c_ref, wg_ref, bg_ref, y_ref, gate_ref, st_ref):
    i = pl.program_id(0)

    @pl.when(i == 0)
    def _():
        st_ref[...] = jnp.zeros_like(st_ref)

    _gate_block(c_ref, wg_ref, bg_ref, gate_ref)
    _pool_tile(i, u_ref, up_ref, gp_ref, band_ref, hband_ref, y_ref)

    a = a_ref[...]
    a_hi = a.astype(BF16).astype(F32)
    lane = lax.broadcasted_iota(jnp.int32, a.shape, 1)
    a3 = jnp.where((lane >= GLA_RANK) & (lane < 2 * GLA_RANK), a - a_hi, a_hi).astype(BF16)
    w = wa_ref[...]
    w_hi = w.astype(BF16).astype(F32)
    wrow = lax.broadcasted_iota(jnp.int32, w.shape, 0)
    w3 = jnp.where(wrow >= 2 * GLA_RANK, w - w_hi, w_hi).astype(BF16)
    xl = (_dot(a3, w3) + ba_ref[...]) * LOG2E
    g2 = (jnp.minimum(xl, 0.0) - jnp.log2(1.0 + jnp.exp2(-jnp.abs(xl)))) * (1.0 / GLA_TAU)
    tcat = tc_ref[...]
    nw = nw_ref[...].astype(BF16)
    for hd in range(GLA_HEADS):
        ks = slice(hd * GLA_DK, (hd + 1) * GLA_DK)
        vs = slice(hd * GLA_DV, (hd + 1) * GLA_DV)
        ys = slice(D_POOL + hd * GLA_DV, D_POOL + (hd + 1) * GLA_DV)
        y_ref[:, ys] = _gla_head(q_ref[:, ks], k_ref[:, ks], v_ref[:, vs], gg_ref[:, vs],
                                 g2[:, ks], nw, tcat, mk_ref, st_ref, hd)


def _mixer(z, a_lr, w_alpha_pad, b_alpha, gla_norm_w, c_act, w_ada, b_ada):
    t = z.shape[0] - Z_LEAD
    n = t // MIX_TR
    assert MIX_TR == GLA_TR and n * MIX_GATE_TN == D_MODEL
    lead = Z_LEAD // MIX_TR
    hpt = MIX_TR // POOL_HALO
    gate0 = 2 * D_MODEL // MIX_GATE_TN
    tcat, masks = _gla_constants()
    bands, hbands = _pool_constants()
    const2 = lambda i: (0, 0)
    const3 = lambda i: (0, 0, 0)
    return pl.pallas_call(
        _mixer_kernel,
        grid=(n,),
        in_specs=[
            pl.BlockSpec((MIX_TR, D_POOL), lambda i: (lead + i, OFF_U // D_POOL)),
            pl.BlockSpec((POOL_HALO, D_POOL), lambda i: ((lead + i) * hpt - 1, OFF_U // D_POOL)),
            pl.BlockSpec((MIX_TR, D_POOL), lambda i: (lead + i, OFF_GP // D_POOL)),
            pl.BlockSpec((MIX_TR, GLA_KEY), lambda i: (lead + i, OFF_Q // GLA_KEY)),
            pl.BlockSpec((MIX_TR, GLA_KEY), lambda i: (lead + i, OFF_K // GLA_KEY)),
            pl.BlockSpec((MIX_TR, D_GLA), lambda i: (lead + i, OFF_V // D_GLA)),
            pl.BlockSpec((MIX_TR, D_GLA), lambda i: (lead + i, OFF_GG // D_GLA)),
            pl.BlockSpec((MIX_TR, RANK_PAD), lambda i: (i, 0)),
            pl.BlockSpec(bands.shape, const3),
            pl.BlockSpec(hbands.shape, const3),
            pl.BlockSpec((RANK_PAD, GLA_KEY), const2),
            pl.BlockSpec((1, GLA_KEY), const2),
            pl.BlockSpec((1, GLA_DV), const2),
            pl.BlockSpec(tcat.shape, const2),
            pl.BlockSpec(masks.shape, const3),
            pl.BlockSpec((D_MODEL, 1), const2),
            pl.BlockSpec((D_MODEL, MIX_GATE_TN), lambda i: (0, gate0 + i)),
            pl.BlockSpec((1, MIX_GATE_TN), lambda i: (0, gate0 + i)),
        ],
        out_specs=[
            pl.BlockSpec((MIX_TR, D_POOL + D_GLA), lambda i: (i, 0)),
            pl.BlockSpec((1, MIX_GATE_TN), lambda i: (0, i)),
        ],
        out_shape=[
            jax.ShapeDtypeStruct((t, D_POOL + D_GLA), BF16),
            jax.ShapeDtypeStruct((1, D_MODEL), F32),
        ],
        scratch_shapes=[pltpu.VMEM((GLA_HEADS, GLA_DV, GLA_DK), F32)],
        compiler_params=_params(1),
        name="mixer",
    )(z, z, z, z, z, z, z, a_lr, bands, hbands, w_alpha_pad, b_alpha,
      gla_norm_w, tcat, masks, c_act, w_ada, b_ada)


OUT_TM = 256
OUT_NC = 512
OUT_VMEM_LIMIT = 63 * 1024 * 1024


def _outproj_kernel(y_ref, w_ref, x_ref, gate_ref, fw_ref, o_ref, xn_scr, inv_scr):
    i = pl.program_id(0)
    last = pl.num_programs(0) - 1
    chunks = [slice(c * OUT_NC, (c + 1) * OUT_NC) for c in range(D_MODEL // OUT_NC)]

    @pl.when(i == 0)
    def _():
        xn_scr[...] = jnp.zeros_like(xn_scr)
        inv_scr[...] = jnp.zeros_like(inv_scr)

    @pl.when(i < last)
    def _():
        inv_prev = inv_scr[...]
        ym = y_ref[...]
        ssq = jnp.zeros((OUT_TM, 1), F32)
        for cs in chunks:
            o_ref[:, cs] = xn_scr[:, cs] * inv_prev * fw_ref[:, cs]
            xn = x_ref[:, cs] + gate_ref[:, cs] * _dot(ym, w_ref[:, cs])
            xn_scr[:, cs] = xn
            ssq = ssq + jnp.sum(xn * xn, axis=-1, keepdims=True)
        inv_scr[...] = lax.rsqrt(ssq * (1.0 / D_MODEL) + EPS)

    @pl.when(i == last)
    def _():
        o_ref[...] = xn_scr[...] * inv_scr[...] * fw_ref[...]


def _outproj(y, w_out_bf, x2, gate, final_norm_w):
    t = x2.shape[0]
    n = t // OUT_TM
    cur = lambda i: (jnp.minimum(i, n - 1), 0)
    return pl.pallas_call(
        _outproj_kernel,
        grid=(n + 1,),
        in_specs=[
            pl.BlockSpec((OUT_TM, D_POOL + D_GLA), cur),
            pl.BlockSpec((D_MODEL, D_MODEL), lambda i: (0, 0), pipeline_mode=pl.Buffered(1)),
            pl.BlockSpec((OUT_TM, D_MODEL), cur),
            pl.BlockSpec((1, D_MODEL), lambda i: (0, 0)),
            pl.BlockSpec((1, D_MODEL), lambda i: (0, 0)),
        ],
        out_specs=pl.BlockSpec((OUT_TM, D_MODEL), lambda i: (jnp.maximum(i - 1, 0), 0)),
        out_shape=jax.ShapeDtypeStruct((t, D_MODEL), F32),
        scratch_shapes=[pltpu.VMEM((OUT_TM, D_MODEL), F32), pltpu.VMEM((OUT_TM, 1), F32)],
        compiler_params=_params(1, OUT_VMEM_LIMIT),
        name="outproj",
    )(y, w_out_bf, x2, gate, final_norm_w)


def kernel(x, c, w_ada, b_ada, norm_w, w_in, w_pool, pool_scale, w_alpha, b_alpha,
           gla_norm_w, w_out, final_norm_w):
    bsz, t, d = x.shape
    assert bsz == 1 and d == D_MODEL and w_ada.shape[0] == 1
    x2 = x.reshape(t, d)

    w_in_t = w_in[0].T
    mod, c_act, w_main = _prelude(c.reshape(d, 1), w_ada[0], b_ada, 2 * d, w_in_t,
                                  w_pool[0], pool_scale)
    w_a = jnp.pad(jnp.tile(w_in_t[D_MAIN:], (3, 1)), ((0, RANK_PAD - 3 * GLA_RANK), (0, 0))).astype(BF16)
    z, a_lr, w_out_bf = _inproj(x2, norm_w, mod, w_a, w_main, w_out[0])

    w_alpha_pad = jnp.pad(jnp.tile(w_alpha[0], (3, 1)), ((0, RANK_PAD - 3 * GLA_RANK), (0, 0)))
    y, gate = _mixer(z, a_lr, w_alpha_pad, b_alpha, gla_norm_w,
                     c_act, w_ada[0], b_ada)

    out = _outproj(y, w_out_bf, x2, gate, final_norm_w.reshape(1, d))
    return out.reshape(bsz, t, d).astype(x.dtype)
```

```python
import math

import jax
import jax.numpy as jnp
import numpy as np
from jax import lax
from jax.experimental import pallas as pl
from jax.experimental.pallas import tpu as pltpu

F32 = jnp.float32
BF16 = jnp.bfloat16

D_MODEL = 4096
D_POOL = 2048
POOL_WINDOWS = (2, 4, 8, 16)
POOL_GROUP = 512
POOL_HALO = 16
D_GLA = 2048
GLA_HEADS = 4
GLA_DV = 512
GLA_DK = 256
GLA_KEY = 1024
GLA_RANK = 16
GLA_TAU = 16.0
EPS = 1e-6
D_MAIN = 2 * D_POOL + 2 * GLA_KEY + 2 * D_GLA
RANK_PAD = 128
LOG2E = math.log2(math.e)

OFF_U, OFF_GP, OFF_Q, OFF_K, OFF_V, OFF_GG = 0, 2048, 4096, 5120, 6144, 8192

VMEM_LIMIT = 56 * 1024 * 1024


def _dot(a, b):
    return jnp.dot(a, b, preferred_element_type=F32)


def _dot_nt(a, b):
    return lax.dot_general(a, b, (((1,), (1,)), ((), ())), preferred_element_type=F32)


def _dot_tn(a, b):
    return lax.dot_general(a, b, (((0,), (0,)), ((), ())), preferred_element_type=F32)


def _split_bf16(a):
    hi = a.astype(BF16)
    lo = (a - hi.astype(F32)).astype(BF16)
    return hi, lo


def _silu(v):
    return v * jax.nn.sigmoid(v)


def _params(n_axes, vmem_limit=VMEM_LIMIT):
    return pltpu.CompilerParams(
        dimension_semantics=("arbitrary",) * n_axes, vmem_limit_bytes=vmem_limit)


ADA_TN = 512
ADA_KC = 256
CAST_TM = 512


def _prelude_kernel(c_ref, w_ref, b_ref, wt_ref, wp_ref, ps_ref, wlr_ref, o_ref, ca_ref, wbf_ref,
                    wa3_ref):
    tn = o_ref.shape[1]
    j = pl.program_id(0)

    @pl.when(j == 0)
    def _():
        ca_row = _silu(c_ref[...])
        ca_ref[...] = jnp.transpose(jnp.broadcast_to(ca_row, (8, D_MODEL)))[:, :1]
        w_lr = wlr_ref[...].astype(BF16)
        wa3_ref[...] = jnp.concatenate(
            [w_lr, w_lr, w_lr, jnp.zeros((RANK_PAD - 3 * GLA_RANK, D_MODEL), BF16)], axis=0)

    @pl.when(j < D_POOL // CAST_TM)
    def _():
        wp = (wp_ref[0] * ps_ref[...]).astype(BF16)
        wbf_ref[...] = _dot_tn(wp, wt_ref[...].astype(BF16)).astype(wbf_ref.dtype)

    @pl.when(j >= D_POOL // CAST_TM)
    def _():
        wbf_ref[...] = wt_ref[...].astype(wbf_ref.dtype)

    def body(kk, acc):
        r = pl.multiple_of(kk * ADA_KC, ADA_KC)
        p = ca_ref[pl.ds(r, ADA_KC), :] * w_ref[pl.ds(r, ADA_KC), :]
        return acc + p.reshape(ADA_KC // 8, 8, tn).sum(axis=0)

    acc = lax.fori_loop(0, D_MODEL // ADA_KC, body, jnp.zeros((8, tn), F32))
    o_ref[...] = acc.sum(axis=0, keepdims=True) + b_ref[...]


def _prelude(c_row, w_ada, b_ada, n, w_t, w_pool, pool_scale):
    n_ada, n_cast = n // ADA_TN, D_MAIN // CAST_TM
    n_grp = D_POOL // CAST_TM
    assert n_cast >= n_ada and CAST_TM == POOL_GROUP and w_pool.shape[0] == n_grp
    ada_blk = lambda j: (0, jnp.minimum(j, n_ada - 1))
    return pl.pallas_call(
        _prelude_kernel,
        grid=(n_cast,),
        in_specs=[
            pl.BlockSpec((1, D_MODEL), lambda j: (0, 0)),
            pl.BlockSpec((D_MODEL, ADA_TN), ada_blk),
            pl.BlockSpec((1, ADA_TN), ada_blk),
            pl.BlockSpec((CAST_TM, w_t.shape[1]), lambda j: (j, 0)),
            pl.BlockSpec((1, POOL_GROUP, POOL_GROUP), lambda j: (jnp.minimum(j, n_grp - 1), 0, 0)),
            pl.BlockSpec((1, POOL_GROUP), lambda j: (0, jnp.minimum(j, n_grp - 1))),
            pl.BlockSpec((GLA_RANK, w_t.shape[1]), lambda j: (D_MAIN // GLA_RANK, 0)),
        ],
        out_specs=[
            pl.BlockSpec((1, ADA_TN), ada_blk),
            pl.BlockSpec((D_MODEL, 1), lambda j: (0, 0)),
            pl.BlockSpec((CAST_TM, w_t.shape[1]), lambda j: (j, 0)),
            pl.BlockSpec((RANK_PAD, w_t.shape[1]), lambda j: (0, 0)),
        ],
        out_shape=[
            jax.ShapeDtypeStruct((1, n), F32),
            jax.ShapeDtypeStruct((D_MODEL, 1), F32),
            jax.ShapeDtypeStruct((D_MAIN, w_t.shape[1]), BF16),
            jax.ShapeDtypeStruct((RANK_PAD, w_t.shape[1]), BF16),
        ],
        compiler_params=_params(1),
        name="prelude",
    )(c_row, w_ada, b_ada, w_t, w_pool, pool_scale, w_t)


IN_TM = 1024
IN_TN = 1280
IN_SLAB = IN_TM // (D_MAIN // IN_TN)
IN_NSPLIT = 768
IN_WO_ROWS = 64
IN_VMEM_LIMIT = 60 * 1024 * 1024
Z_LEAD = IN_TM


def _inproj_step(h_src, h_dst, j, x_ref, nw_ref, shift_ref, scale_ref, wa_ref, w_ref, wo_ref,
                 z_ref, a_ref, wo_bf_ref):
    xs = x_ref[...]
    ms = jnp.mean(xs * xs, axis=-1, keepdims=True)
    mul = nw_ref[...] * (1.0 + scale_ref[...])
    h = (xs * lax.rsqrt(ms + EPS) * mul + shift_ref[...]).astype(BF16)
    h_dst[pl.ds(pl.multiple_of(j * IN_SLAB, IN_SLAB), IN_SLAB), :] = h
    if h_src is None:
        z_ref[...] = jnp.zeros_like(z_ref)
        a_ref[...] = _dot_nt(h, wa_ref[...])
    else:
        hs = h_src[...]
        z_ref[:, :IN_NSPLIT] = _dot_nt(hs, w_ref[:IN_NSPLIT, :]).astype(z_ref.dtype)
        a_ref[...] = _dot_nt(h, wa_ref[...])
        z_ref[:, IN_NSPLIT:] = _dot_nt(hs, w_ref[IN_NSPLIT:, :]).astype(z_ref.dtype)
    wo_bf_ref[...] = wo_ref[...].astype(wo_bf_ref.dtype)


def _inproj_kernel(x_ref, nw_ref, shift_ref, scale_ref, wa_ref, w_ref, wo_ref,
                   z_ref, a_ref, wo_bf_ref, h_even, h_odd):
    i = pl.program_id(0)
    j = pl.program_id(1)
    refs = (x_ref, nw_ref, shift_ref, scale_ref, wa_ref, w_ref, wo_ref, z_ref, a_ref, wo_bf_ref)
    pl.when(i == 0)(lambda: _inproj_step(None, h_even, j, *refs))
    pl.when((i & 1) == 1)(lambda: _inproj_step(h_even, h_odd, j, *refs))
    pl.when(jnp.logical_and(i > 0, (i & 1) == 0))(lambda: _inproj_step(h_odd, h_even, j, *refs))


def _inproj(x2, norm_w, mod, w_a, w_main, w_out):
    t = x2.shape[0]
    n_i, n_j = t // IN_TM, D_MAIN // IN_TN
    n_wo = w_out.shape[0] // IN_WO_ROWS
    assert IN_SLAB * n_j == IN_TM and (n_i + 1) * n_j >= n_wo

    def x_slab(i, j):
        return (jnp.minimum(i, n_i - 1) * n_j + j, 0)

    def wo_slab(i, j):
        return (jnp.minimum(i * n_j + j, n_wo - 1), 0)

    return pl.pallas_call(
        _inproj_kernel,
        grid=(n_i + 1, n_j),
        in_specs=[
            pl.BlockSpec((IN_SLAB, D_MODEL), x_slab),
            pl.BlockSpec((1, D_MODEL), lambda i, j: (0, 0)),
            pl.BlockSpec((1, D_MODEL), lambda i, j: (0, 0)),
            pl.BlockSpec((1, D_MODEL), lambda i, j: (0, 1)),
            pl.BlockSpec((RANK_PAD, D_MODEL), lambda i, j: (0, 0)),
            pl.BlockSpec((IN_TN, D_MODEL), lambda i, j: (jnp.where(i == 0, 0, j), 0)),
            pl.BlockSpec((IN_WO_ROWS, w_out.shape[1]), wo_slab),
        ],
        out_specs=[
            pl.BlockSpec((IN_TM, IN_TN), lambda i, j: (i, j)),
            pl.BlockSpec((IN_SLAB, RANK_PAD), lambda i, j: (i * n_j + j, 0)),
            pl.BlockSpec((IN_WO_ROWS, w_out.shape[1]), wo_slab),
        ],
        out_shape=[
            jax.ShapeDtypeStruct((Z_LEAD + t, D_MAIN), BF16),
            jax.ShapeDtypeStruct((t + IN_TM, RANK_PAD), F32),
            jax.ShapeDtypeStruct(w_out.shape, BF16),
        ],
        scratch_shapes=[pltpu.VMEM((IN_TM, D_MODEL), BF16), pltpu.VMEM((IN_TM, D_MODEL), BF16)],
        compiler_params=_params(2, IN_VMEM_LIMIT),
        name="inproj",
    )(x2, norm_w, mod, mod, w_a, w_main, w_out)


MIX_TR = 256


def _pool_constants():
    r = np.arange(MIX_TR)[:, None]
    c = np.arange(MIX_TR)[None, :]
    hr = np.arange(POOL_HALO)[:, None]
    hc = np.arange(POOL_HALO)[None, :]
    bands, hbands = [], []
    for w in POOL_WINDOWS:
        bands.append(np.where((c <= r) & (c > r - w), 1.0 / w, 0.0) - (r == c))
        hbands.append(np.where(hr + POOL_HALO - hc < w, 1.0 / w, 0.0))
    return (jnp.asarray(np.stack(bands), BF16), jnp.asarray(np.stack(hbands), BF16))


def _pool_tile(i, u_ref, up_ref, gp_ref, band_ref, hband_ref, y_ref):
    pad = jnp.zeros((MIX_TR - POOL_HALO, POOL_GROUP), F32)
    rows = lax.broadcasted_iota(jnp.int32, (POOL_HALO, 1), 0)
    for g, w in enumerate(POOL_WINDOWS):
        cs = slice(g * POOL_GROUP, (g + 1) * POOL_GROUP)
        u = u_ref[:, cs]
        pooled = _dot(band_ref[g], u) + jnp.concatenate([_dot(hband_ref[g], up_ref[:, cs]), pad], axis=0)
        ratio = jnp.where(i == 0, float(w) / jnp.minimum(rows + 1, w).astype(F32), 1.0)
        u_head = u[:POOL_HALO].astype(F32)
        head = ratio * (pooled[:POOL_HALO] + u_head) - u_head
        pooled = jnp.concatenate([head, pooled[POOL_HALO:]], axis=0)
        y_ref[:, cs] = pooled.astype(BF16) * _silu(gp_ref[:, cs])


GLA_TR = 256
GLA_LEVELS = GLA_TR.bit_length() - 1
GLA_MXU_LEVELS = 4


def _gla_constants():
    r = np.arange(GLA_TR)[:, None]
    t = np.arange(GLA_TR)[None, :]
    mats = []
    for k in range(1, GLA_MXU_LEVELS):
        half = ((r >> (k + 1)) << (k + 1)) + (1 << k)
        second = ((r >> k) & 1) == 1
        mats.append(np.where(second, (t >= half) & (t <= r), (t > r) & (t < half)))
    mats.append(t <= r)
    tcat = np.concatenate(mats, axis=0).astype(np.float32)
    diff = np.maximum(r ^ t, 1)
    level = np.where(t == r, GLA_LEVELS, np.where(t < r, np.floor(np.log2(diff)).astype(np.int64), -1))
    masks = np.stack([(level == k) for k in range(GLA_LEVELS + 1)]).astype(np.float32) * GLA_DK ** -0.5
    return jnp.asarray(tcat, BF16), jnp.asarray(masks, BF16)


def _gla_head(q_bf, k_bf, v, gg, g2, nw, tcat, mk_ref, st_ref, hd):
    tr, nl = GLA_TR, GLA_LEVELS
    g_hi, g_lo = _split_bf16(g2)
    n_small = (GLA_MXU_LEVELS - 1) * tr
    dall = _dot(tcat[:n_small], g_hi)
    b2 = _dot(tcat[n_small:], g_hi) + _dot(tcat[n_small:], g_lo)
    rbit = lax.broadcasted_iota(jnp.int32, (tr, 1), 0)

    a_bf = mk_ref[nl] * _dot_nt(q_bf, k_bf).astype(BF16)
    for k in range(nl):
        if k == 0:
            m = jnp.where((rbit & 1) == 1, q_bf * jnp.exp2(g2).astype(BF16), k_bf)
        elif k < GLA_MXU_LEVELS:
            side = jnp.where(((rbit >> k) & 1) == 1, q_bf, k_bf)
            m = side * jnp.exp2(dall[(k - 1) * tr:k * tr]).astype(BF16)
        else:
            s = 1 << k
            parts = []
            for j in range(tr // (2 * s)):
                lo = 2 * s * j
                bp = b2[lo + s - 1:lo + s]
                parts.append(k_bf[lo:lo + s] * jnp.exp2(bp - b2[lo:lo + s]).astype(BF16))
                parts.append(q_bf[lo + s:lo + 2 * s] * jnp.exp2(b2[lo + s:lo + 2 * s] - bp).astype(BF16))
            m = jnp.concatenate(parts, axis=0)
        a_bf = a_bf + mk_ref[k] * _dot_nt(m, m).astype(BF16)

    b_end = b2[tr - 1:tr]
    q_in = q_bf * jnp.exp2(b2).astype(BF16)
    k_out = k_bf * jnp.exp2((b_end - 0.5 * math.log2(GLA_DK)) - b2).astype(BF16)
    st = st_ref[hd]
    o = _dot(a_bf, v) + _dot_nt(q_in, st.astype(BF16))
    st_ref[hd] = jnp.exp2(b_end) * st + _dot_tn(v, k_out)

    ms = jnp.mean(o * o, axis=-1, keepdims=True)
    on = (o * lax.rsqrt(ms + EPS)).astype(BF16)
    return on * (_silu(gg) * nw)


MIX_GATE_TN = 128
MIX_GATE_KC = 256


def _gate_block(c_ref, wg_ref, bg_ref, gate_ref):
    acc = jnp.zeros((8, MIX_GATE_TN), F32)
    for kk in range(D_MODEL // MIX_GATE_KC):
        rs = slice(kk * MIX_GATE_KC, (kk + 1) * MIX_GATE_KC)
        p = c_ref[rs, :] * wg_ref[rs, :]
        acc = acc + p.reshape(MIX_GATE_KC // 8, 8, MIX_GATE_TN).sum(axis=0)
    gate_ref[...] = acc.sum(axis=0, keepdims=True) + bg_ref[...]


def _mixer_kernel(u_ref, up_ref, gp_ref, q_ref, k_ref, v_ref, gg_ref, a_ref,
                  band_ref, hband_ref, wa_ref, ba_ref, nw_ref, tc_ref, mk_ref,
                  c_ref, wg_ref, bg_ref, y_ref, gate_ref, st_ref):
    i = pl.program_id(0)

    @pl.when(i == 0)
    def _():
        st_ref[...] = jnp.zeros_like(st_ref)

    _gate_block(c_ref, wg_ref, bg_ref, gate_ref)
    _pool_tile(i, u_ref, up_ref, gp_ref, band_ref, hband_ref, y_ref)

    a = a_ref[...]
    a_hi = a.astype(BF16).astype(F32)
    lane = lax.broadcasted_iota(jnp.int32, a.shape, 1)
    a3 = jnp.where((lane >= GLA_RANK) & (lane < 2 * GLA_RANK), a - a_hi, a_hi).astype(BF16)
    w_al = wa_ref[...]
    w = jnp.concatenate([w_al, w_al, w_al, jnp.zeros((RANK_PAD - 3 * GLA_RANK, GLA_KEY), F32)], axis=0)
    w_hi = w.astype(BF16).astype(F32)
    wrow = lax.broadcasted_iota(jnp.int32, w.shape, 0)
    w3 = jnp.where(wrow >= 2 * GLA_RANK, w - w_hi, w_hi).astype(BF16)
    xl = (_dot(a3, w3) + ba_ref[...]) * LOG2E
    g2 = (jnp.minimum(xl, 0.0) - jnp.log2(1.0 + jnp.exp2(-jnp.abs(xl)))) * (1.0 / GLA_TAU)
    tcat = tc_ref[...]
    nw = nw_ref[...].astype(BF16)
    for hd in range(GLA_HEADS):
        ks = slice(hd * GLA_DK, (hd + 1) * GLA_DK)
        vs = slice(hd * GLA_DV, (hd + 1) * GLA_DV)
        ys = slice(D_POOL + hd * GLA_DV, D_POOL + (hd + 1) * GLA_DV)
        y_ref[:, ys] = _gla_head(q_ref[:, ks], k_ref[:, ks], v_ref[:, vs], gg_ref[:, vs],
                                 g2[:, ks], nw, tcat, mk_ref, st_ref, hd)


def _mixer(z, a_lr, w_alpha, b_alpha, gla_norm_w, c_act, w_ada, b_ada):
    t = z.shape[0] - Z_LEAD
    n = t // MIX_TR
    assert MIX_TR == GLA_TR and n * MIX_GATE_TN == D_MODEL
    lead = Z_LEAD // MIX_TR
    hpt = MIX_TR // POOL_HALO
    gate0 = 2 * D_MODEL // MIX_GATE_TN
    tcat, masks = _gla_constants()
    bands, hbands = _pool_constants()
    const2 = lambda i: (0, 0)
    const3 = lambda i: (0, 0, 0)
    return pl.pallas_call(
        _mixer_kernel,
        grid=(n,),
        in_specs=[
            pl.BlockSpec((MIX_TR, D_POOL), lambda i: (lead + i, OFF_U // D_POOL)),
            pl.BlockSpec((POOL_HALO, D_POOL), lambda i: ((lead + i) * hpt - 1, OFF_U // D_POOL)),
            pl.BlockSpec((MIX_TR, D_POOL), lambda i: (lead + i, OFF_GP // D_POOL)),
            pl.BlockSpec((MIX_TR, GLA_KEY), lambda i: (lead + i, OFF_Q // GLA_KEY)),
            pl.BlockSpec((MIX_TR, GLA_KEY), lambda i: (lead + i, OFF_K // GLA_KEY)),
            pl.BlockSpec((MIX_TR, D_GLA), lambda i: (lead + i, OFF_V // D_GLA)),
            pl.BlockSpec((MIX_TR, D_GLA), lambda i: (lead + i, OFF_GG // D_GLA)),
            pl.BlockSpec((MIX_TR, RANK_PAD), lambda i: (i, 0)),
            pl.BlockSpec(bands.shape, const3),
            pl.BlockSpec(hbands.shape, const3),
            pl.BlockSpec((GLA_RANK, GLA_KEY), const2),
            pl.BlockSpec((1, GLA_KEY), const2),
            pl.BlockSpec((1, GLA_DV), const2),
            pl.BlockSpec(tcat.shape, const2),
            pl.BlockSpec(masks.shape, const3),
            pl.BlockSpec((D_MODEL, 1), const2),
            pl.BlockSpec((D_MODEL, MIX_GATE_TN), lambda i: (0, gate0 + i)),
            pl.BlockSpec((1, MIX_GATE_TN), lambda i: (0, gate0 + i)),
        ],
        out_specs=[
            pl.BlockSpec((MIX_TR, D_POOL + D_GLA), lambda i: (i, 0)),
            pl.BlockSpec((1, MIX_GATE_TN), lambda i: (0, i)),
        ],
        out_shape=[
            jax.ShapeDtypeStruct((t, D_POOL + D_GLA), BF16),
            jax.ShapeDtypeStruct((1, D_MODEL), F32),
        ],
        scratch_shapes=[pltpu.VMEM((GLA_HEADS, GLA_DV, GLA_DK), F32)],
        compiler_params=_params(1),
        name="mixer",
    )(z, z, z, z, z, z, z, a_lr, bands, hbands, w_alpha, b_alpha,
      gla_norm_w, tcat, masks, c_act, w_ada, b_ada)


OUT_TM = 256
OUT_NC = 512
OUT_VMEM_LIMIT = 63 * 1024 * 1024


def _outproj_kernel(y_ref, w_ref, x_ref, gate_ref, fw_ref, o_ref, xn_scr, inv_scr):
    i = pl.program_id(0)
    last = pl.num_programs(0) - 1
    chunks = [slice(c * OUT_NC, (c + 1) * OUT_NC) for c in range(D_MODEL // OUT_NC)]

    @pl.when(i == 0)
    def _():
        xn_scr[...] = jnp.zeros_like(xn_scr)
        inv_scr[...] = jnp.zeros_like(inv_scr)

    @pl.when(i < last)
    def _():
        inv_prev = inv_scr[...]
        ym = y_ref[...]
        ssq = jnp.zeros((OUT_TM, 1), F32)
        for cs in chunks:
            o_ref[:, cs] = xn_scr[:, cs] * inv_prev * fw_ref[:, cs]
            xn = x_ref[:, cs] + gate_ref[:, cs] * _dot(ym, w_ref[:, cs])
            xn_scr[:, cs] = xn
            ssq = ssq + jnp.sum(xn * xn, axis=-1, keepdims=True)
        inv_scr[...] = lax.rsqrt(ssq * (1.0 / D_MODEL) + EPS)

    @pl.when(i == last)
    def _():
        o_ref[...] = xn_scr[...] * inv_scr[...] * fw_ref[...]


def _outproj(y, w_out_bf, x2, gate, final_norm_w):
    t = x2.shape[0]
    n = t // OUT_TM
    cur = lambda i: (jnp.minimum(i, n - 1), 0)
    return pl.pallas_call(
        _outproj_kernel,
        grid=(n + 1,),
        in_specs=[
            pl.BlockSpec((OUT_TM, D_POOL + D_GLA), cur),
            pl.BlockSpec((D_MODEL, D_MODEL), lambda i: (0, 0), pipeline_mode=pl.Buffered(1)),
            pl.BlockSpec((OUT_TM, D_MODEL), cur),
            pl.BlockSpec((1, D_MODEL), lambda i: (0, 0)),
            pl.BlockSpec((1, D_MODEL), lambda i: (0, 0)),
        ],
        out_specs=pl.BlockSpec((OUT_TM, D_MODEL), lambda i: (jnp.maximum(i - 1, 0), 0)),
        out_shape=jax.ShapeDtypeStruct((t, D_MODEL), F32),
        scratch_shapes=[pltpu.VMEM((OUT_TM, D_MODEL), F32), pltpu.VMEM((OUT_TM, 1), F32)],
        compiler_params=_params(1, OUT_VMEM_LIMIT),
        name="outproj",
    )(y, w_out_bf, x2, gate, final_norm_w)


def kernel(x, c, w_ada, b_ada, norm_w, w_in, w_pool, pool_scale, w_alpha, b_alpha,
           gla_norm_w, w_out, final_norm_w):
    bsz, t, d = x.shape
    assert bsz == 1 and d == D_MODEL and w_ada.shape[0] == 1
    x2 = x.reshape(t, d)

    w_in_t = w_in[0].T
    mod, c_act, w_main, w_a = _prelude(c, w_ada[0], b_ada, 2 * d, w_in_t,
                                       w_pool[0], pool_scale)
    z, a_lr, w_out_bf = _inproj(x2, norm_w, mod, w_a, w_main, w_out[0])

    y, gate = _mixer(z, a_lr, w_alpha[0], b_alpha, gla_norm_w,
                     c_act, w_ada[0], b_ada)

    out = _outproj(y, w_out_bf, x2, gate, final_norm_w.reshape(1, d))
    return out.reshape(bsz, t, d).astype(x.dtype)
```
